```python
import jax, jax.numpy as jnp
from jax import lax
import numpy as np

D_MODEL = 1024
BATCH = 16
SEQ = 256
DEPTH = 2
DEC_BATCH = 8
DEC_SEQ = 4096
PAST_LEN = 512

GRID_W = 64
N_HEADS = 8
KV_HEADS = 2
HEAD_DIM = 64
Q_PER_KV = N_HEADS // KV_HEADS
ATT_W = N_HEADS * HEAD_DIM
KV_W = KV_HEADS * HEAD_DIM
WINDOW = 128
BLOCK = 128
CONV_W = D_MODEL // 2
CONV_K = 3
FOU_GROUPS = 4
FOU_GW = 128
FOU_W = FOU_GROUPS * FOU_GW
BRANCH_W = 512
N_BRANCH = 3
ROT_AXIS = HEAD_DIM // 2
ROPE_BASE = 10000.0
LN_EPS = 1e-6
NEG = -1e30
DEEPNORM_ALPHA = (2 * DEPTH) ** 0.25
DEEPNORM_BETA = (8 * DEPTH) ** -0.25
IN_SPLITS = (ATT_W, KV_W, KV_W, ATT_W, CONV_W, CONV_W, CONV_W, CONV_W, FOU_W, FOU_W, N_BRANCH * D_MODEL)
IN_W = 2 * ATT_W + 2 * KV_W + 4 * CONV_W + 2 * FOU_W + N_BRANCH * D_MODEL

kernel_name = "hybrid_diffusion_gated_branches_step"


def _layernorm(x):
    xf = x.astype(jnp.float32)
    mu = jnp.mean(xf, axis=-1, keepdims=True)
    var = jnp.mean(jnp.square(xf - mu), axis=-1, keepdims=True)
    return (xf - mu) * lax.rsqrt(var + LN_EPS)


def _modulation(cond, w_mod_l, b_mod_l):
    m = jax.nn.silu(cond) @ w_mod_l + b_mod_l
    shift, scale, gate = jnp.split(m, 3, axis=-1)
    return shift[:, None, :], scale[:, None, :], gate[:, None, :]


def _grid_angles(n_tokens):
    rows = n_tokens // GRID_W
    row = jnp.repeat(jnp.arange(rows, dtype=jnp.int32), GRID_W).astype(jnp.float32)
    col = jnp.tile(jnp.arange(GRID_W, dtype=jnp.int32), rows).astype(jnp.float32)
    n_freq = ROT_AXIS // 2
    inv_freq = ROPE_BASE ** (-jnp.arange(n_freq, dtype=jnp.float32) / n_freq)
    return row[:, None] * inv_freq, col[:, None] * inv_freq


def _rotate(seg, ang):
    x1, x2 = jnp.split(seg, 2, axis=-1)
    shape = (1, seg.shape[1]) + (1,) * (seg.ndim - 3) + (ang.shape[-1],)
    cos = jnp.cos(ang).reshape(shape).astype(seg.dtype)
    sin = jnp.sin(ang).reshape(shape).astype(seg.dtype)
    return jnp.concatenate([x1 * cos - x2 * sin, x2 * cos + x1 * sin], axis=-1)


def _rope2d(x, ang_r, ang_c):
    return jnp.concatenate([_rotate(x[..., :ROT_AXIS], ang_r), _rotate(x[..., ROT_AXIS:], ang_c)], axis=-1)


def _softmax_with_sink(logits, sink_l):
    sink = jnp.broadcast_to(sink_l.astype(jnp.float32).reshape(1, KV_HEADS, Q_PER_KV, 1, 1),
                            logits.shape[:-1] + (1,))
    p = jax.nn.softmax(jnp.concatenate([logits, sink], axis=-1), axis=-1)
    return p[..., :-1]


def _attend_context(q, k, v, sink_l):
    b, s = q.shape[:2]
    logits = jnp.einsum('bqkgd,bckd->bkgqc', q, k).astype(jnp.float32) * (HEAD_DIM ** -0.5)
    p = _softmax_with_sink(logits, sink_l).astype(v.dtype)
    out = jnp.einsum('bkgqc,bckd->bqkgd', p, v)
    return out.reshape(b, s, ATT_W)


def _attend_latent(q, k, v, k_ctx, v_ctx, sink_l):
    b, s = q.shape[:2]
    nb = s // BLOCK
    pad = ((0, 0), (BLOCK, BLOCK), (0, 0), (0, 0))
    kp = jnp.pad(k, pad)
    vp = jnp.pad(v, pad)
    scale = HEAD_DIM ** -0.5

    def block(i):
        start = i * BLOCK
        qb = lax.dynamic_slice_in_dim(q, start, BLOCK, axis=1)
        kb = lax.dynamic_slice_in_dim(kp, start, 3 * BLOCK, axis=1)
        vb = lax.dynamic_slice_in_dim(vp, start, 3 * BLOCK, axis=1)
        s_loc = jnp.einsum('bqkgd,bjkd->bkgqj', qb, kb).astype(jnp.float32) * scale
        tq = start + jnp.arange(BLOCK)
        tk = start - BLOCK + jnp.arange(3 * BLOCK)
        valid = (tk[None, :] >= 0) & (tk[None, :] < s) & (jnp.abs(tq[:, None] - tk[None, :]) <= WINDOW)
        s_loc = jnp.where(valid, s_loc, NEG)
        s_ctx = jnp.einsum('bqkgd,bckd->bkgqc', qb, k_ctx).astype(jnp.float32) * scale
        p = _softmax_with_sink(jnp.concatenate([s_loc, s_ctx], axis=-1), sink_l).astype(v.dtype)
        return (jnp.einsum('bkgqj,bjkd->bqkgd', p[..., :3 * BLOCK], vb)
                + jnp.einsum('bkgqc,bckd->bqkgd', p[..., 3 * BLOCK:], v_ctx))

    outs = lax.map(block, jnp.arange(nb))
    return jnp.moveaxis(outs, 0, 1).reshape(b, s, ATT_W)


def _short_conv(b_gate, c_gate, xin, conv_w_l):
    u = c_gate * xin
    s = u.shape[1]
    up = jnp.pad(u, ((0, 0), (1, 1), (0, 0)))
    y = up[:, 0:s] * conv_w_l[0] + up[:, 1:s + 1] * conv_w_l[1] + up[:, 2:s + 2] * conv_w_l[2]
    return b_gate * y


def _fourier(f):
    b, s, _ = f.shape
    g = f.reshape(b, s, FOU_GROUPS, FOU_GW).astype(jnp.float32)
    y = jnp.fft.fft2(g, axes=(1, 3), norm='ortho').real
    return y.astype(f.dtype).reshape(b, s, FOU_W)


def _layer(x, shift, scale, gate, w_in_l, conv_w_l, w_branch_l, w_o_l, ln_g_l, ln_b_l, attn_fn):
    b, s, _ = x.shape
    h = (_layernorm(x) * (1.0 + scale) + shift).astype(x.dtype)
    offsets = np.cumsum(IN_SPLITS)[:-1].tolist()
    q, k, v, z_a, cb, cc, cx, z_b, fx, z_c, g = jnp.split(h @ w_in_l, offsets, axis=-1)
    q = q.reshape(b, s, KV_HEADS, Q_PER_KV, HEAD_DIM)
    k = k.reshape(b, s, KV_HEADS, HEAD_DIM)
    v = v.reshape(b, s, KV_HEADS, HEAD_DIM)
    y_a = attn_fn(q, k, v) * jax.nn.silu(z_a)
    y_b = _short_conv(cb, cc, cx, conv_w_l) * jax.nn.silu(z_b)
    y_c = _fourier(fx) * jax.nn.silu(z_c)
    gates = jax.nn.sigmoid(g.reshape(b, s, N_BRANCH, D_MODEL))
    merged = (gates[:, :, 0] * (y_a @ w_branch_l[0])
              + gates[:, :, 1] * (y_b @ w_branch_l[1])
              + gates[:, :, 2] * (y_c @ w_branch_l[2]))
    out = merged @ w_o_l
    r = _layernorm(DEEPNORM_ALPHA * x + gate * out) * ln_g_l + ln_b_l
    return r.astype(x.dtype), k, v


def setup_inputs(seed: int = 0) -> dict:
    key = jax.random.key(seed)
    ks = jax.random.split(key, 16)
    nrm = jax.random.normal
    f32 = jnp.float32
    return {
        'x_prompt': nrm(ks[0], (BATCH, SEQ, D_MODEL), f32),
        'x_sample': nrm(ks[1], (DEC_BATCH, DEC_SEQ, D_MODEL), f32),
        'cache_k': nrm(ks[2], (DEC_BATCH, DEPTH, PAST_LEN, KV_HEADS, HEAD_DIM), f32),
        'cache_v': nrm(ks[3], (DEC_BATCH, DEPTH, PAST_LEN, KV_HEADS, HEAD_DIM), f32),
        'c': nrm(ks[4], (DEC_BATCH, D_MODEL), f32),
        'c_ctx': nrm(ks[5], (D_MODEL,), f32),
        'w_mod': nrm(ks[6], (DEPTH, D_MODEL, 3 * D_MODEL), f32) * D_MODEL ** -0.5,
        'b_mod': 0.01 * nrm(ks[7], (DEPTH, 3 * D_MODEL), f32),
        'w_in': nrm(ks[8], (DEPTH, D_MODEL, IN_W), f32) * D_MODEL ** -0.5,
        'sink': 0.5 * nrm(ks[9], (DEPTH, N_HEADS), f32),
        'conv_w': nrm(ks[10], (DEPTH, CONV_K, CONV_W), f32) * CONV_K ** -0.5,
        'w_branch': nrm(ks[11], (DEPTH, N_BRANCH, BRANCH_W, D_MODEL), f32) * (BRANCH_W ** -0.5 * DEEPNORM_BETA),
        'w_o': nrm(ks[12], (DEPTH, D_MODEL, D_MODEL), f32) * (D_MODEL ** -0.5 * DEEPNORM_BETA),
        'ln_g': 1.0 + 0.01 * nrm(ks[13], (DEPTH, D_MODEL), f32),
        'ln_b': 0.01 * nrm(ks[14], (DEPTH, D_MODEL), f32),
    }


def reference(x_prompt, x_sample, cache_k, cache_v, c, c_ctx, w_mod, b_mod, w_in, sink, conv_w,
              w_branch, w_o, ln_g, ln_b):
    y_prompt = x_prompt
    k_list, v_list = [], []
    for l in range(DEPTH):
        shift, scale, gate = _modulation(c_ctx[None, :], w_mod[l], b_mod[l])
        attn = lambda q, k, v, l=l: _attend_context(q, k, v, sink[l])
        y_prompt, k_l, v_l = _layer(y_prompt, shift, scale, gate, w_in[l], conv_w[l], w_branch[l],
                                    w_o[l], ln_g[l], ln_b[l], attn)
        k_list.append(k_l)
        v_list.append(v_l)
    new_k = jnp.stack(k_list, axis=1)
    new_v = jnp.stack(v_list, axis=1)

    y_sample = x_sample
    ang_r, ang_c = _grid_angles(x_sample.shape[1])
    for l in range(DEPTH):
        shift, scale, gate = _modulation(c, w_mod[l], b_mod[l])
        attn = lambda q, k, v, l=l: _attend_latent(_rope2d(q, ang_r, ang_c), _rope2d(k, ang_r, ang_c), v,
                                                   cache_k[:, l], cache_v[:, l], sink[l])
        y_sample, _, _ = _layer(y_sample, shift, scale, gate, w_in[l], conv_w[l], w_branch[l],
                                w_o[l], ln_g[l], ln_b[l], attn)

    return (y_prompt, y_sample, new_k, new_v)
```

```python
import functools

import numpy as np
import jax
import jax.numpy as jnp
from jax import lax
from jax.experimental import pallas as pl
from jax.experimental.pallas import tpu as pltpu

F32 = jnp.float32
BF16 = jnp.bfloat16

D_MODEL = 1024
GRID_W = 64
N_HEADS = 8
KV_HEADS = 2
HEAD_DIM = 64
Q_PER_KV = N_HEADS // KV_HEADS
ATT_W = N_HEADS * HEAD_DIM
KV_W = KV_HEADS * HEAD_DIM
WINDOW = 128
CONV_W = 512
FOU_GROUPS = 4
FOU_GW = 128
FOU_W = FOU_GROUPS * FOU_GW
BRANCH_W = 512
N_BRANCH = 3
ROT_AXIS = HEAD_DIM // 2
ROPE_BASE = 10000.0
LN_EPS = 1e-6
NEG = -1e30

_OFF = np.cumsum([0, ATT_W, KV_W, KV_W, ATT_W, CONV_W, CONV_W, CONV_W, CONV_W, FOU_W, FOU_W, N_BRANCH * D_MODEL])
(O_Q, O_K, O_V, O_ZA, O_CB, O_CC, O_CX, O_ZB, O_FX, O_ZC, O_G, IN_W) = [int(v) for v in _OFF]

W1_COLS = 2 * KV_W + 2 * CONV_W + FOU_W
W2_COLS = 2 * ATT_W + 2 * CONV_W + FOU_W + N_BRANCH * D_MODEL
P2_Q, P2_ZA, P2_CB, P2_ZB, P2_ZC, P2_G = 0, 512, 1024, 1536, 2048, 2560

V7X_VMEM_BYTES = 64 * 1024 * 1024
LANES = 128
SUBLANES = 8

TOKEN_TILE = 256
MOD_COL_TILE = 768


def _layernorm(x):
    mu = jnp.mean(x, axis=-1, keepdims=True)
    xc = x - mu
    var = jnp.mean(xc * xc, axis=-1, keepdims=True)
    return xc * lax.rsqrt(var + LN_EPS)


def _silu(z):
    return z * jax.nn.sigmoid(z)


def _rope(x, cos, sin_signed):
    lane = lax.broadcasted_iota(jnp.int32, x.shape, 1)
    partner = jnp.where((lane & 16) == 0, pltpu.roll(x, LANES - 16, 1), pltpu.roll(x, 16, 1))
    return x * cos + partner * sin_signed


def _mod_kernel(c_ref, w_ref, b_ref, o_ref):
    c = c_ref[...]
    a = _silu(c).astype(BF16)
    o_ref[0] = jnp.dot(a, w_ref[0].astype(BF16), preferred_element_type=F32) + b_ref[0]


def _modulation(cond, w_mod, b_mod):
    depth, d, n = w_mod.shape
    rows = cond.shape[0]
    return pl.pallas_call(
        _mod_kernel,
        grid=(depth, n // MOD_COL_TILE),
        in_specs=[
            pl.BlockSpec((rows, d), lambda l, j: (0, 0)),
            pl.BlockSpec((1, d, MOD_COL_TILE), lambda l, j: (l, 0, j)),
            pl.BlockSpec((1, 1, MOD_COL_TILE), lambda l, j: (l, 0, j)),
        ],
        out_specs=pl.BlockSpec((1, rows, MOD_COL_TILE), lambda l, j: (l, 0, j)),
        out_shape=jax.ShapeDtypeStruct((depth, rows, n), F32),
        compiler_params=pltpu.CompilerParams(dimension_semantics=("parallel", "parallel")),
        name="modulation",
    )(cond, w_mod, b_mod.reshape(depth, 1, n))


def _p1_kernel(latent, *refs):
    if latent:
        x_ref, mod_ref, w1_ref, dftc_ref, cos_ref, sin_ref, k_ref, v_ref, u_ref, z_ref = refs
    else:
        x_ref, mod_ref, w1_ref, dftc_ref, k_ref, v_ref, u_ref, z_ref = refs
    x = x_ref[0]
    shift = mod_ref[0, 0, 0:1, :]
    scale = mod_ref[0, 0, 1:2, :]
    hb = (_layernorm(x) * (1.0 + scale) + shift).astype(BF16)
    p = jnp.dot(hb, w1_ref[0], preferred_element_type=F32)
    k = p[:, 0:KV_W]
    if latent:
        k = _rope(k, cos_ref[...], sin_ref[...])
    k_ref[0] = k
    v_ref[0] = p[:, KV_W:2 * KV_W]
    o = 2 * KV_W
    u_ref[0] = p[:, o:o + CONV_W] * p[:, o + CONV_W:o + 2 * CONV_W]
    o += 2 * CONV_W
    for g in range(FOU_GROUPS):
        fg = p[:, o + g * FOU_GW:o + (g + 1) * FOU_GW].astype(BF16)
        zg = jnp.dot(fg, dftc_ref[...], preferred_element_type=F32)
        z_ref[0, 0, :, g * FOU_GW:(g + 1) * FOU_GW] = zg[:, :FOU_GW].astype(BF16)
        z_ref[0, 1, :, g * FOU_GW:(g + 1) * FOU_GW] = zg[:, FOU_GW:].astype(BF16)


def _pass1(x, mod, mod_row, layer, w1, dftc, rope_tabs, latent):
    b, s, d = x.shape
    t = min(TOKEN_TILE, s)
    nt = s // t
    if mod_row is None:
        mod_idx = lambda i, j: (layer, i, 0, 0)
    else:
        mod_idx = lambda i, j: (layer, mod_row, 0, 0)
    in_specs = [
        pl.BlockSpec((1, t, d), lambda i, j: (i, j, 0)),
        pl.BlockSpec((1, 1, 3, d), mod_idx),
        pl.BlockSpec((1, d, W1_COLS), lambda i, j: (layer, 0, 0)),
        pl.BlockSpec((FOU_GW, 2 * FOU_GW), lambda i, j: (0, 0)),
    ]
    args = [x, mod, w1, dftc]
    if latent:
        in_specs += [pl.BlockSpec((t, KV_W), lambda i, j: (j, 0))] * 2
        args += list(rope_tabs)
    out_shape = [
        jax.ShapeDtypeStruct((b, s, KV_W), F32),
        jax.ShapeDtypeStruct((b, s, KV_W), F32),
        jax.ShapeDtypeStruct((b, s, CONV_W), F32),
        jax.ShapeDtypeStruct((b, 2, s, FOU_W), BF16),
    ]
    out_specs = [
        pl.BlockSpec((1, t, KV_W), lambda i, j: (i, j, 0)),
        pl.BlockSpec((1, t, KV_W), lambda i, j: (i, j, 0)),
        pl.BlockSpec((1, t, CONV_W), lambda i, j: (i, j, 0)),
        pl.BlockSpec((1, 2, t, FOU_W), lambda i, j: (i, 0, j, 0)),
    ]
    return pl.pallas_call(
        functools.partial(_p1_kernel, latent),
        grid=(b, nt),
        in_specs=in_specs,
        out_specs=out_specs,
        out_shape=out_shape,
        compiler_params=pltpu.CompilerParams(
            dimension_semantics=("parallel", "parallel"),
            vmem_limit_bytes=V7X_VMEM_BYTES // 2),
        name="pass1_latent" if latent else "pass1_context",
    )(*args)


def _dft_kernel(tk, t_ref, z_ref, o_ref, acc_ref):
    kk = pl.program_id(2)

    @pl.when(kk == 0)
    def _():
        acc_ref[...] = jnp.zeros_like(acc_ref)

    zs = z_ref[0, pl.ds(pl.multiple_of(kk * tk, tk), tk), :]
    acc_ref[...] += jnp.dot(t_ref[...], zs, preferred_element_type=F32)

    @pl.when(kk == pl.num_programs(2) - 1)
    def _():
        o_ref[0] = acc_ref[...]


def _position_dft(tmat, z):
    b, s2, w = z.shape
    s = s2 // 2
    tm = min(1024, s)
    tk = min(2048, s2)
    return pl.pallas_call(
        functools.partial(_dft_kernel, tk),
        grid=(b, s // tm, s2 // tk),
        in_specs=[
            pl.BlockSpec((tm, tk), lambda i, m, k: (m, k)),
            pl.BlockSpec((1, s2, w), lambda i, m, k: (i, 0, 0)),
        ],
        out_specs=pl.BlockSpec((1, tm, w), lambda i, m, k: (i, m, 0)),
        out_shape=jax.ShapeDtypeStruct((b, s, w), F32),
        scratch_shapes=[pltpu.VMEM((tm, w), F32)],
        compiler_params=pltpu.CompilerParams(
            dimension_semantics=("parallel", "parallel", "arbitrary"),
            vmem_limit_bytes=V7X_VMEM_BYTES // 2),
        name="position_dft",
    )(tmat, z)


def _attend(qpad, keys, vals, masks, sink_h):
    scores = []
    for kb, mk in zip(keys, masks):
        s = lax.dot_general(qpad, kb, (((1,), (1,)), ((), ())), preferred_element_type=F32)
        if mk is not None:
            s = jnp.where(mk, s, NEG)
        scores.append(s)
    m = sink_h
    for s in scores:
        m = jnp.maximum(m, jnp.max(s, axis=-1, keepdims=True))
    den = jnp.exp(sink_h - m)
    acc = None
    for s, vb in zip(scores, vals):
        e = jnp.exp(s - m)
        den = den + jnp.sum(e, axis=-1, keepdims=True)
        pv = jnp.dot(e.astype(BF16), vb, preferred_element_type=F32)
        acc = pv if acc is None else acc + pv
    return acc / den


def _p2_kernel(latent, alpha, seq_len, *refs):
    if latent:
        (x_ref, mod_ref, w2_ref, wb_ref, wo_ref, cw_ref, lng_ref, lnb_ref, sink_ref, yf_ref,
         um_ref, up_ref, un_ref, km_ref, kp_ref, kn_ref, vm_ref, vp_ref, vn_ref,
         kc_ref, vc_ref, cos_ref, sin_ref, y_ref) = refs
    else:
        (x_ref, mod_ref, w2_ref, wb_ref, wo_ref, cw_ref, lng_ref, lnb_ref, sink_ref, yf_ref,
         um_ref, km_ref, vm_ref, y_ref) = refs
    t = pl.program_id(1)
    nt = pl.num_programs(1)
    x = x_ref[0]
    tq = x.shape[0]
    shift = mod_ref[0, 0, 0:1, :]
    scale = mod_ref[0, 0, 1:2, :]
    gate = mod_ref[0, 0, 2:3, :]
    hb = (_layernorm(x) * (1.0 + scale) + shift).astype(BF16)

    def proj(off, width):
        return jnp.dot(hb, w2_ref[0, :, off:off + width], preferred_element_type=F32)

    q = proj(P2_Q, ATT_W)
    if latent:
        keys = [jnp.concatenate([kp_ref[0], km_ref[0], kn_ref[0]], axis=0).astype(BF16),
                kc_ref[0, 0].astype(BF16)]
        vals = [jnp.concatenate([vp_ref[0], vm_ref[0], vn_ref[0]], axis=0).astype(BF16),
                vc_ref[0, 0].astype(BF16)]
        nloc = keys[0].shape[0]
        r = lax.broadcasted_iota(jnp.int32, (tq, nloc), 0)
        j = lax.broadcasted_iota(jnp.int32, (tq, nloc), 1)
        dlt = j - r
        kpos = j + (t * tq - WINDOW)
        band = (dlt >= 0) & (dlt <= 2 * WINDOW) & (kpos >= 0) & (kpos < seq_len)
        masks = [band, None]
        cos = cos_ref[...]
        sin = sin_ref[...]
    else:
        keys = [km_ref[0].astype(BF16)]
        vals = [vm_ref[0].astype(BF16)]
        masks = [None]
    lane = lax.broadcasted_iota(jnp.int32, (tq, LANES), 1)
    low = lane < HEAD_DIM
    ya_chunks = []
    for g in range(Q_PER_KV):
        qc = q[:, g * LANES:(g + 1) * LANES]
        if latent:
            qc = _rope(qc, cos, sin)
        qc = qc * (HEAD_DIM ** -0.5)
        q_lo = jnp.where(low, qc, 0.0).astype(BF16)
        q_hi = jnp.where(low, 0.0, qc).astype(BF16)
        o_lo = _attend(q_lo, keys, vals, masks, sink_ref[g:g + 1, 0:1])
        o_hi = _attend(q_hi, keys, vals, masks, sink_ref[Q_PER_KV + g:Q_PER_KV + g + 1, 0:1])
        ya_chunks.append(jnp.where(low, o_lo, o_hi))
    ya = jnp.concatenate(ya_chunks, axis=-1)
    y_a = ya * _silu(proj(P2_ZA, ATT_W))

    u = um_ref[0]
    if latent:
        prev_row = jnp.where(t > 0, up_ref[0, SUBLANES - 1:SUBLANES, :], 0.0)
        next_row = jnp.where(t < nt - 1, un_ref[0, 0:1, :], 0.0)
    else:
        prev_row = jnp.zeros((1, CONV_W), F32)
        next_row = jnp.zeros((1, CONV_W), F32)
    row = lax.broadcasted_iota(jnp.int32, u.shape, 0)
    u_up = jnp.where(row == 0, prev_row, pltpu.roll(u, 1, 0))
    u_dn = jnp.where(row == tq - 1, next_row, pltpu.roll(u, tq - 1, 0))
    conv = u_up * cw_ref[0, 0:1, :] + u * cw_ref[0, 1:2, :] + u_dn * cw_ref[0, 2:3, :]
    y_b = proj(P2_CB, CONV_W) * conv * _silu(proj(P2_ZB, CONV_W))

    y_c = yf_ref[0] * _silu(proj(P2_ZC, FOU_W))

    merged = None
    for br, yb in enumerate((y_a, y_b, y_c)):
        gt = jax.nn.sigmoid(proj(P2_G + br * D_MODEL, D_MODEL))
        term = gt * jnp.dot(yb.astype(BF16), wb_ref[0, br], preferred_element_type=F32)
        merged = term if merged is None else merged + term
    out = jnp.dot(merged.astype(BF16), wo_ref[0], preferred_element_type=F32)
    res = alpha * x + gate * out
    y_ref[0] = _layernorm(res) * lng_ref[0] + lnb_ref[0]


def _pass2(x, mod, mod_row, layer, w2, wb, wo, conv_w, ln_g, ln_b, sink_b, yf, u, k, v,
           cache_k, cache_v, rope_tabs, latent, alpha):
    b, s, d = x.shape
    t = min(TOKEN_TILE, s)
    nt = s // t
    if mod_row is None:
        mod_idx = lambda i, j: (layer, i, 0, 0)
    else:
        mod_idx = lambda i, j: (layer, mod_row, 0, 0)
    const3 = lambda i, j: (layer, 0, 0)
    tile3 = lambda i, j: (i, j, 0)
    in_specs = [
        pl.BlockSpec((1, t, d), tile3),
        pl.BlockSpec((1, 1, 3, d), mod_idx),
        pl.BlockSpec((1, d, W2_COLS), const3),
        pl.BlockSpec((1, N_BRANCH, BRANCH_W, d), lambda i, j: (layer, 0, 0, 0)),
        pl.BlockSpec((1, d, d), const3),
        pl.BlockSpec((1, 3, CONV_W), const3),
        pl.BlockSpec((1, 1, d), const3),
        pl.BlockSpec((1, 1, d), const3),
        pl.BlockSpec((N_HEADS, LANES), lambda i, j: (layer, 0)),
        pl.BlockSpec((1, t, FOU_W), tile3),
        pl.BlockSpec((1, t, CONV_W), tile3),
    ]
    args = [x, mod, w2, wb, wo, conv_w, ln_g, ln_b, sink_b, yf, u]
    if latent:
        assert t == 2 * WINDOW and s % t == 0
        rpt = t // SUBLANES
        nb8 = s // SUBLANES
        nbw = s // WINDOW
        prev8 = lambda i, j: (i, jnp.maximum(j * rpt - 1, 0), 0)
        next8 = lambda i, j: (i, jnp.minimum((j + 1) * rpt, nb8 - 1), 0)
        prevw = lambda i, j: (i, jnp.maximum(2 * j - 1, 0), 0)
        nextw = lambda i, j: (i, jnp.minimum(2 * j + 2, nbw - 1), 0)
        in_specs += [pl.BlockSpec((1, SUBLANES, CONV_W), prev8), pl.BlockSpec((1, SUBLANES, CONV_W), next8)]
        args += [u, u]
        for arr in (k, v):
            in_specs += [pl.BlockSpec((1, t, KV_W), tile3),
                         pl.BlockSpec((1, WINDOW, KV_W), prevw),
                         pl.BlockSpec((1, WINDOW, KV_W), nextw)]
            args += [arr, arr, arr]
        past = cache_k.shape[2]
        cache_spec = pl.BlockSpec((1, 1, past, KV_W), lambda i, j: (i, layer, 0, 0))
        in_specs += [cache_spec, cache_spec]
        args += [cache_k, cache_v]
        in_specs += [pl.BlockSpec((t, KV_W), lambda i, j: (j, 0))] * 2
        args += list(rope_tabs)
    else:
        assert nt == 1
        in_specs += [pl.BlockSpec((1, t, KV_W), tile3)] * 2
        args += [k, v]
    return pl.pallas_call(
        functools.partial(_p2_kernel, latent, alpha, s),
        grid=(b, nt),
        in_specs=in_specs,
        out_specs=pl.BlockSpec((1, t, d), tile3),
        out_shape=jax.ShapeDtypeStruct((b, s, d), F32),
        compiler_params=pltpu.CompilerParams(
            dimension_semantics=("parallel", "parallel"),
            vmem_limit_bytes=V7X_VMEM_BYTES * 7 // 8),
        name="pass2_latent" if latent else "pass2_context",
    )(*args)


def _dft_tables(n):
    idx = jnp.arange(n, dtype=jnp.int32)
    if n <= 256:
        ang = (2.0 * np.pi / n) * ((idx[:, None] * idx[None, :]) % n).astype(F32)
        return jnp.cos(ang), jnp.sin(ang)
    r = 64
    hi = jnp.arange(n // r, dtype=jnp.int32)
    lo = jnp.arange(r, dtype=jnp.int32)
    a = (2.0 * np.pi / (n // r)) * ((hi[:, None] * idx[None, :]) % (n // r)).astype(F32)
    bb = (2.0 * np.pi / n) * ((lo[:, None] * idx[None, :]) % n).astype(F32)
    ca, sa = jnp.cos(a)[:, None, :], jnp.sin(a)[:, None, :]
    cb, sb = jnp.cos(bb)[None, :, :], jnp.sin(bb)[None, :, :]
    return (ca * cb - sa * sb).reshape(n, n), (sa * cb + ca * sb).reshape(n, n)


def _position_matrix(s):
    c, sn = _dft_tables(s)
    return (jnp.concatenate([c, -sn], axis=1) * (s ** -0.5)).astype(BF16)


def _channel_matrix():
    c, sn = _dft_tables(FOU_GW)
    return (jnp.concatenate([c, sn], axis=1) * (FOU_GW ** -0.5)).astype(BF16)


def _rope_tables(n_tokens):
    pos = jnp.arange(n_tokens, dtype=jnp.int32)
    row = (pos // GRID_W).astype(F32)
    col = (pos % GRID_W).astype(F32)
    n_freq = ROT_AXIS // 2
    inv_freq = ROPE_BASE ** (-jnp.arange(n_freq, dtype=F32) / n_freq)
    lane = np.arange(LANES)
    dd = lane % HEAD_DIM
    fidx = jnp.asarray(dd % n_freq)
    use_row = jnp.asarray(dd < ROT_AXIS)
    ang = jnp.where(use_row[None, :], row[:, None], col[:, None]) * inv_freq[fidx][None, :]
    sign = jnp.asarray(np.where((lane % ROT_AXIS) < n_freq, -1.0, 1.0), F32)
    return jnp.cos(ang), jnp.sin(ang) * sign[None, :]


def _head_perm():
    return np.arange(ATT_W).reshape(KV_HEADS, Q_PER_KV, HEAD_DIM).transpose(1, 0, 2).reshape(ATT_W)


def kernel(x_prompt, x_sample, cache_k, cache_v, c, c_ctx, w_mod, b_mod, w_in, sink, conv_w, w_branch, w_o, ln_g, ln_b):
    depth = w_in.shape[0]
    alpha = float((2 * depth) ** 0.25)
    dec_b, dec_s, d = x_sample.shape
    ctx_b, ctx_s, _ = x_prompt.shape

    rows = -(-(dec_b + 1) // SUBLANES) * SUBLANES
    cond = jnp.zeros((rows, d), F32).at[:dec_b].set(c).at[dec_b].set(c_ctx)
    mod = _modulation(cond, w_mod, b_mod).reshape(depth, rows, 3, d)

    perm = _head_perm()
    sl = lambda a, bnd: w_in[:, :, a:bnd]
    w1 = jnp.concatenate([sl(O_K, O_V), sl(O_V, O_ZA), sl(O_CC, O_CX), sl(O_CX, O_ZB), sl(O_FX, O_ZC)],
                         axis=-1).astype(BF16)
    w2 = jnp.concatenate([sl(O_Q, O_K)[:, :, perm], sl(O_ZA, O_CB)[:, :, perm], sl(O_CB, O_CC),
                          sl(O_ZB, O_FX), sl(O_ZC, O_G), sl(O_G, IN_W)], axis=-1).astype(BF16)
    wb = jnp.concatenate([w_branch[:, 0:1][:, :, perm, :], w_branch[:, 1:]], axis=1).astype(BF16)
    wo = w_o.astype(BF16)
    sink_b = jnp.broadcast_to(sink.reshape(depth * N_HEADS, 1), (depth * N_HEADS, LANES))
    ln_g3 = ln_g.reshape(depth, 1, d)
    ln_b3 = ln_b.reshape(depth, 1, d)

    dftc = _channel_matrix()
    rope_tabs = _rope_tables(dec_s)
    past = cache_k.shape[2]
    ck = cache_k.reshape(dec_b, depth, past, KV_W)
    cv = cache_v.reshape(dec_b, depth, past, KV_W)

    def run_group(x, mod_row, latent):
        b, s, _ = x.shape
        tmat = _position_matrix(s)
        y = x
        ks, vs = [], []
        for l in range(depth):
            k, v, u, z = _pass1(y, mod, mod_row, l, w1, dftc, rope_tabs, latent)
            yf = _position_dft(tmat, z.reshape(b, 2 * s, FOU_W))
            y = _pass2(y, mod, mod_row, l, w2, wb, wo, conv_w, ln_g3, ln_b3, sink_b, yf, u, k, v,
                       ck, cv, rope_tabs, latent, alpha)
            ks.append(k)
            vs.append(v)
        return y, ks, vs

    y_prompt, ks, vs = run_group(x_prompt, dec_b, False)
    new_k = jnp.stack(ks, axis=1).reshape(ctx_b, depth, ctx_s, KV_HEADS, HEAD_DIM)
    new_v = jnp.stack(vs, axis=1).reshape(ctx_b, depth, ctx_s, KV_HEADS, HEAD_DIM)
    y_sample, _, _ = run_group(x_sample, None, True)
    return (y_prompt, y_sample, new_k, new_v)
```

```python
import functools

import numpy as np
import jax
import jax.numpy as jnp
from jax import lax
from jax.experimental import pallas as pl
from jax.experimental.pallas import tpu as pltpu

F32 = jnp.float32
BF16 = jnp.bfloat16

D_MODEL = 1024
GRID_W = 64
N_HEADS = 8
KV_HEADS = 2
HEAD_DIM = 64
Q_PER_KV = N_HEADS // KV_HEADS
ATT_W = N_HEADS * HEAD_DIM
KV_W = KV_HEADS * HEAD_DIM
WINDOW = 128
CONV_W = 512
FOU_GROUPS = 4
FOU_GW = 128
FOU_W = FOU_GROUPS * FOU_GW
BRANCH_W = 512
N_BRANCH = 3
ROT_AXIS = HEAD_DIM // 2
ROPE_BASE = 10000.0
LN_EPS = 1e-6
NEG = -1e30
LOG2E = 1.4426950408889634

_OFF = np.cumsum([0, ATT_W, KV_W, KV_W, ATT_W, CONV_W, CONV_W, CONV_W, CONV_W, FOU_W, FOU_W, N_BRANCH * D_MODEL])
(O_Q, O_K, O_V, O_ZA, O_CB, O_CC, O_CX, O_ZB, O_FX, O_ZC, O_G, IN_W) = [int(v) for v in _OFF]

W1_COLS = 2 * KV_W + 2 * CONV_W + FOU_W
W2_COLS = 2 * ATT_W + 2 * CONV_W + FOU_W + N_BRANCH * D_MODEL
P2_Q, P2_ZA, P2_CB, P2_ZB, P2_ZC, P2_G = 0, 512, 1024, 1536, 2048, 2560

V7X_VMEM_BYTES = 64 * 1024 * 1024
LANES = 128
SUBLANES = 8

P1_TOKEN_TILE = 512
TOKEN_TILE = 256
REV_BLOCK = 128
MOD_COL_TILE = 768


def _layernorm(x):
    mu = jnp.mean(x, axis=-1, keepdims=True)
    xc = x - mu
    var = jnp.mean(xc * xc, axis=-1, keepdims=True)
    return xc * lax.rsqrt(var + LN_EPS)


def _silu(z):
    return z * jax.nn.sigmoid(z)


def _rope(x, cos, sin_signed):
    lane = lax.broadcasted_iota(jnp.int32, x.shape, 1)
    partner = jnp.where((lane & 16) == 0, pltpu.roll(x, LANES - 16, 1), pltpu.roll(x, 16, 1))
    return x * cos + partner * sin_signed


def _mod_kernel(c_ref, w_ref, b_ref, o_ref):
    c = c_ref[...]
    a = _silu(c).astype(BF16)
    o_ref[0] = jnp.dot(a, w_ref[0].astype(BF16), preferred_element_type=F32) + b_ref[0]


def _modulation(cond, w_mod, b_mod):
    depth, d, n = w_mod.shape
    rows = cond.shape[0]
    return pl.pallas_call(
        _mod_kernel,
        grid=(depth, n // MOD_COL_TILE),
        in_specs=[
            pl.BlockSpec((rows, d), lambda l, j: (0, 0)),
            pl.BlockSpec((1, d, MOD_COL_TILE), lambda l, j: (l, 0, j)),
            pl.BlockSpec((1, 1, MOD_COL_TILE), lambda l, j: (l, 0, j)),
        ],
        out_specs=pl.BlockSpec((1, rows, MOD_COL_TILE), lambda l, j: (l, 0, j)),
        out_shape=jax.ShapeDtypeStruct((depth, rows, n), F32),
        compiler_params=pltpu.CompilerParams(dimension_semantics=("parallel", "parallel")),
        name="modulation",
    )(cond, w_mod, b_mod.reshape(depth, 1, n))


def _p1_kernel(latent, *refs):
    if latent:
        x_ref, mod_ref, w1_ref, dftc_ref, cos_ref, sin_ref, k_ref, v_ref, u_ref, z_ref = refs
    else:
        x_ref, mod_ref, w1_ref, dftc_ref, k_ref, v_ref, u_ref, z_ref = refs
    x = x_ref[0]
    shift = mod_ref[0, 0, 0:1, :]
    scale = mod_ref[0, 0, 1:2, :]
    hb = (_layernorm(x) * (1.0 + scale) + shift).astype(BF16)
    p = jnp.dot(hb, w1_ref[0], preferred_element_type=F32)
    k = p[:, 0:KV_W]
    if latent:
        k = _rope(k, cos_ref[...], sin_ref[...])
    k_ref[0] = k
    v_ref[0] = p[:, KV_W:2 * KV_W]
    o = 2 * KV_W
    u_ref[0] = p[:, o:o + CONV_W] * p[:, o + CONV_W:o + 2 * CONV_W]
    o += 2 * CONV_W
    for g in range(FOU_GROUPS):
        fg = p[:, o + g * FOU_GW:o + (g + 1) * FOU_GW].astype(BF16)
        zg = jnp.dot(fg, dftc_ref[...], preferred_element_type=F32)
        z_ref[0, 0, :, g * FOU_GW:(g + 1) * FOU_GW] = zg[:, :FOU_GW].astype(BF16)
        z_ref[0, 1, :, g * FOU_GW:(g + 1) * FOU_GW] = zg[:, FOU_GW:].astype(BF16)


def _pass1(x, mod, mod_row, layer, w1, dftc, rope_tabs, latent):
    b, s, d = x.shape
    t = min(P1_TOKEN_TILE, s)
    assert s % t == 0
    nt = s // t
    if mod_row is None:
        mod_idx = lambda i, j: (layer, i, 0, 0)
    else:
        mod_idx = lambda i, j: (layer, mod_row, 0, 0)
    in_specs = [
        pl.BlockSpec((1, t, d), lambda i, j: (i, j, 0)),
        pl.BlockSpec((1, 1, 3, d), mod_idx),
        pl.BlockSpec((1, d, W1_COLS), lambda i, j: (layer, 0, 0)),
        pl.BlockSpec((FOU_GW, 2 * FOU_GW), lambda i, j: (0, 0)),
    ]
    args = [x, mod, w1, dftc]
    if latent:
        in_specs += [pl.BlockSpec((t, KV_W), lambda i, j: (j, 0))] * 2
        args += list(rope_tabs)
    out_shape = [
        jax.ShapeDtypeStruct((b, s, KV_W), F32),
        jax.ShapeDtypeStruct((b, s, KV_W), F32),
        jax.ShapeDtypeStruct((b, s, CONV_W), F32),
        jax.ShapeDtypeStruct((b, 2, s, FOU_W), BF16),
    ]
    out_specs = [
        pl.BlockSpec((1, t, KV_W), lambda i, j: (i, j, 0)),
        pl.BlockSpec((1, t, KV_W), lambda i, j: (i, j, 0)),
        pl.BlockSpec((1, t, CONV_W), lambda i, j: (i, j, 0)),
        pl.BlockSpec((1, 2, t, FOU_W), lambda i, j: (i, 0, j, 0)),
    ]
    return pl.pallas_call(
        functools.partial(_p1_kernel, latent),
        grid=(b, nt),
        in_specs=in_specs,
        out_specs=out_specs,
        out_shape=out_shape,
        compiler_params=pltpu.CompilerParams(
            dimension_semantics=("parallel", "parallel"),
            vmem_limit_bytes=V7X_VMEM_BYTES // 2),
        name="pass1_latent" if latent else "pass1_context",
    )(*args)


def _dft_kernel(n, tm, z_ref, ch_ref, sh_ref, rev_ref, o_ref, e_ref, od_ref, ab_ref, zh_ref):
    h = n // 2
    blk = rev_ref.shape[0]
    nblk = n // blk
    mi = pl.program_id(1)
    inv = n ** -0.5
    rev = rev_ref[...]

    @pl.when(mi == 0)
    def _fold():
        r = lax.broadcasted_iota(jnp.int32, (blk, 1), 0)
        sgn = jnp.where((r & 1) == 0, 1.0, -1.0)
        asum = jnp.zeros((1, e_ref.shape[1]), F32)
        for i in range(h // blk):
            q = nblk - 1 - i
            if i == 0:
                pad = jnp.zeros((blk, e_ref.shape[1]), BF16)
                mir_r = jnp.concatenate([z_ref[0, q * blk:(q + 1) * blk, :], pad], axis=0)
                mir_i = jnp.concatenate([z_ref[0, n + q * blk:n + (q + 1) * blk, :], pad], axis=0)
            else:
                mir_r = z_ref[0, q * blk:(q + 2) * blk, :]
                mir_i = z_ref[0, n + q * blk:n + (q + 2) * blk, :]
            e = z_ref[0, i * blk:(i + 1) * blk, :].astype(F32) + jnp.dot(rev, mir_r, preferred_element_type=F32)
            o = (z_ref[0, n + i * blk:n + (i + 1) * blk, :].astype(F32)
                 - jnp.dot(rev, mir_i, preferred_element_type=F32))
            e_ref[i * blk:(i + 1) * blk, :] = e.astype(BF16)
            od_ref[i * blk:(i + 1) * blk, :] = o.astype(BF16)
            asum = asum + jnp.sum(e * sgn, axis=0, keepdims=True)
        zh = z_ref[0, h:h + 1, :].astype(F32) * inv
        zh_ref[...] = zh
        ab_ref[h:h + blk, :] = jnp.where(r == 0, asum * inv + zh, 0.0).astype(BF16)

    a = jnp.dot(ch_ref[...], e_ref[...], preferred_element_type=F32)
    b = jnp.dot(sh_ref[...], od_ref[...], preferred_element_type=F32)
    rr = lax.broadcasted_iota(jnp.int32, (tm, 1), 0)
    a = a + jnp.where((rr & 1) == 0, 1.0, -1.0) * zh_ref[...]
    rows = pl.ds(pl.multiple_of(mi * tm, tm), tm)
    o_ref[0, rows, :] = (a - b).astype(o_ref.dtype)
    ab_ref[rows, :] = (a + b).astype(BF16)

    @pl.when(mi == pl.num_programs(1) - 1)
    def _unfold():
        for p in range(h // blk):
            q = h // blk - p - 1
            win = ab_ref[q * blk:(q + 2) * blk, :]
            o_ref[0, h + p * blk:h + (p + 1) * blk, :] = jnp.dot(
                rev, win, preferred_element_type=F32).astype(o_ref.dtype)


def _position_dft(ch, sh, rev, z):
    b, s2, w = z.shape
    s = s2 // 2
    h = s // 2
    blk = rev.shape[0]
    assert h % blk == 0 and rev.shape[1] == 2 * blk
    tm = min(512, h)
    return pl.pallas_call(
        functools.partial(_dft_kernel, s, tm),
        grid=(b, h // tm),
        in_specs=[
            pl.BlockSpec((1, s2, w), lambda i, m: (i, 0, 0)),
            pl.BlockSpec((tm, h), lambda i, m: (m, 0)),
            pl.BlockSpec((tm, h), lambda i, m: (m, 0)),
            pl.BlockSpec((blk, 2 * blk), lambda i, m: (0, 0)),
        ],
        out_specs=pl.BlockSpec((1, s, w), lambda i, m: (i, 0, 0)),
        out_shape=jax.ShapeDtypeStruct((b, s, w), BF16),
        scratch_shapes=[pltpu.VMEM((h, w), BF16), pltpu.VMEM((h, w), BF16),
                        pltpu.VMEM((h + blk, w), BF16), pltpu.VMEM((1, w), F32)],
        compiler_params=pltpu.CompilerParams(
            dimension_semantics=("parallel", "arbitrary"),
            vmem_limit_bytes=V7X_VMEM_BYTES * 3 // 4),
        name="position_dft",
    )(z, ch, sh, rev)


def _scores(qpad, keys, masks):
    scores = []
    for kb, mk in zip(keys, masks):
        s = lax.dot_general(qpad, kb, (((1,), (1,)), ((), ())), preferred_element_type=F32)
        if mk is not None:
            s = jnp.where(mk, s, NEG)
        scores.append(s)
    return scores


def _softmax_pv(scores, vals_aug, sink2):
    m = sink2
    for s in scores:
        m = jnp.maximum(m, jnp.max(s, axis=-1, keepdims=True))
    acc = None
    for s, vb in zip(scores, vals_aug):
        pv = jnp.dot(jnp.exp2(s - m).astype(BF16), vb, preferred_element_type=F32)
        acc = pv if acc is None else acc + pv
    return acc, jnp.exp2(sink2 - m)


def _p2_kernel(latent, alpha, seq_len, *refs):
    if latent:
        (x_ref, mod_ref, w2_ref, wb_ref, wo_ref, cw_ref, lng_ref, lnb_ref, sink_ref, yf_ref,
         um_ref, up_ref, un_ref, km_ref, kp_ref, kn_ref, vm_ref, vp_ref, vn_ref,
         kc_ref, vc_ref, cos_ref, sin_ref, y_ref) = refs
    else:
        (x_ref, mod_ref, w2_ref, wb_ref, wo_ref, cw_ref, lng_ref, lnb_ref, sink_ref, yf_ref,
         um_ref, km_ref, vm_ref, y_ref) = refs
    t = pl.program_id(1)
    nt = pl.num_programs(1)
    x = x_ref[0]
    tq = x.shape[0]
    shift = mod_ref[0, 0, 0:1, :]
    scale = mod_ref[0, 0, 1:2, :]
    gate = mod_ref[0, 0, 2:3, :]
    hb = (_layernorm(x) * (1.0 + scale) + shift).astype(BF16)

    def proj(off, width):
        return jnp.dot(hb, w2_ref[0, :, off:off + width], preferred_element_type=F32)

    q = proj(P2_Q, ATT_W)
    if latent:
        keys = [jnp.concatenate([kp_ref[0], km_ref[0], kn_ref[0]], axis=0).astype(BF16),
                kc_ref[0, 0].astype(BF16)]
        vals = [jnp.concatenate([vp_ref[0], vm_ref[0], vn_ref[0]], axis=0).astype(BF16),
                vc_ref[0, 0].astype(BF16)]
        nloc = keys[0].shape[0]
        r = lax.broadcasted_iota(jnp.int32, (tq, nloc), 0)
        j = lax.broadcasted_iota(jnp.int32, (tq, nloc), 1)
        dlt = j - r
        kpos = j + (t * tq - WINDOW)
        band = (dlt >= 0) & (dlt <= 2 * WINDOW) & (kpos >= 0) & (kpos < seq_len)
        masks = [band, None]
        cos = cos_ref[...]
        sin = sin_ref[...]
    else:
        keys = [km_ref[0].astype(BF16)]
        vals = [vm_ref[0].astype(BF16)]
        masks = [None]
    lane = lax.broadcasted_iota(jnp.int32, (tq, LANES), 1)
    low = lane < HEAD_DIM
    one = jnp.ones((), BF16)
    vals_lo, vals_hi = [], []
    for vb in vals:
        low_k = lax.broadcasted_iota(jnp.int32, vb.shape, 1) < HEAD_DIM
        vals_lo.append(jnp.where(low_k, vb, one))
        vals_hi.append(jnp.where(low_k, one, vb))
    heads = []
    for g in range(Q_PER_KV):
        qc = q[:, g * LANES:(g + 1) * LANES]
        if latent:
            qc = _rope(qc, cos, sin)
        qc = qc * (HEAD_DIM ** -0.5 * LOG2E)
        heads.append((jnp.where(low, qc, 0.0).astype(BF16), vals_lo, sink_ref[g:g + 1, 0:1] * LOG2E))
        heads.append((jnp.where(low, 0.0, qc).astype(BF16), vals_hi,
                      sink_ref[Q_PER_KV + g:Q_PER_KV + g + 1, 0:1] * LOG2E))
    results = []
    sc = _scores(heads[0][0], keys, masks)
    for i, (_, vaug, sink2) in enumerate(heads):
        nxt = _scores(heads[i + 1][0], keys, masks) if i + 1 < len(heads) else None
        results.append(_softmax_pv(sc, vaug, sink2))
        sc = nxt
    ya_chunks = []
    for g in range(Q_PER_KV):
        (a_lo, t_lo), (a_hi, t_hi) = results[2 * g], results[2 * g + 1]
        num = jnp.where(low, a_lo, a_hi)
        den = pltpu.roll(jnp.where(low, a_hi, a_lo), HEAD_DIM, 1) + jnp.where(low, t_lo, t_hi)
        ya_chunks.append(num / den)
    ya = jnp.concatenate(ya_chunks, axis=-1)
    y_a = ya * _silu(proj(P2_ZA, ATT_W))

    u = um_ref[0]
    if latent:
        prev_row = jnp.where(t > 0, up_ref[0, SUBLANES - 1:SUBLANES, :], 0.0)
        next_row = jnp.where(t < nt - 1, un_ref[0, 0:1, :], 0.0)
    else:
        prev_row = jnp.zeros((1, CONV_W), F32)
        next_row = jnp.zeros((1, CONV_W), F32)
    row = lax.broadcasted_iota(jnp.int32, u.shape, 0)
    u_up = jnp.where(row == 0, prev_row, pltpu.roll(u, 1, 0))
    u_dn = jnp.where(row == tq - 1, next_row, pltpu.roll(u, tq - 1, 0))
    conv = u_up * cw_ref[0, 0:1, :] + u * cw_ref[0, 1:2, :] + u_dn * cw_ref[0, 2:3, :]
    y_b = proj(P2_CB, CONV_W) * conv * _silu(proj(P2_ZB, CONV_W))

    y_c = yf_ref[0].astype(F32) * _silu(proj(P2_ZC, FOU_W))

    merged = None
    for br, yb in enumerate((y_a, y_b, y_c)):
        gt = jax.nn.sigmoid(proj(P2_G + br * D_MODEL, D_MODEL))
        term = gt * jnp.dot(yb.astype(BF16), wb_ref[0, br], preferred_element_type=F32)
        merged = term if merged is None else merged + term
    out = jnp.dot(merged.astype(BF16), wo_ref[0], preferred_element_type=F32)
    res = alpha * x + gate * out
    y_ref[0] = _layernorm(res) * lng_ref[0] + lnb_ref[0]


def _pass2(x, mod, mod_row, layer, w2, wb, wo, conv_w, ln_g, ln_b, sink_b, yf, u, k, v,
           cache_k, cache_v, rope_tabs, latent, alpha):
    b, s, d = x.shape
    t = min(TOKEN_TILE, s)
    nt = s // t
    if mod_row is None:
        mod_idx = lambda i, j: (layer, i, 0, 0)
    else:
        mod_idx = lambda i, j: (layer, mod_row, 0, 0)
    const3 = lambda i, j: (layer, 0, 0)
    tile3 = lambda i, j: (i, j, 0)
    in_specs = [
        pl.BlockSpec((1, t, d), tile3),
        pl.BlockSpec((1, 1, 3, d), mod_idx),
        pl.BlockSpec((1, d, W2_COLS), const3),
        pl.BlockSpec((1, N_BRANCH, BRANCH_W, d), lambda i, j: (layer, 0, 0, 0)),
        pl.BlockSpec((1, d, d), const3),
        pl.BlockSpec((1, 3, CONV_W), const3),
        pl.BlockSpec((1, 1, d), const3),
        pl.BlockSpec((1, 1, d), const3),
        pl.BlockSpec((N_HEADS, LANES), lambda i, j: (layer, 0)),
        pl.BlockSpec((1, t, FOU_W), tile3),
        pl.BlockSpec((1, t, CONV_W), tile3),
    ]
    args = [x, mod, w2, wb, wo, conv_w, ln_g, ln_b, sink_b, yf, u]
    if latent:
        assert t == 2 * WINDOW and s % t == 0
        rpt = t // SUBLANES
        nb8 = s // SUBLANES
        nbw = s // WINDOW
        prev8 = lambda i, j: (i, jnp.maximum(j * rpt - 1, 0), 0)
        next8 = lambda i, j: (i, jnp.minimum((j + 1) * rpt, nb8 - 1), 0)
        prevw = lambda i, j: (i, jnp.maximum(2 * j - 1, 0), 0)
        nextw = lambda i, j: (i, jnp.minimum(2 * j + 2, nbw - 1), 0)
        in_specs += [pl.BlockSpec((1, SUBLANES, CONV_W), prev8), pl.BlockSpec((1, SUBLANES, CONV_W), next8)]
        args += [u, u]
        for arr in (k, v):
            in_specs += [pl.BlockSpec((1, t, KV_W), tile3),
                         pl.BlockSpec((1, WINDOW, KV_W), prevw),
                         pl.BlockSpec((1, WINDOW, KV_W), nextw)]
            args += [arr, arr, arr]
        past = cache_k.shape[2]
        cache_spec = pl.BlockSpec((1, 1, past, KV_W), lambda i, j: (i, layer, 0, 0))
        in_specs += [cache_spec, cache_spec]
        args += [cache_k, cache_v]
        in_specs += [pl.BlockSpec((t, KV_W), lambda i, j: (j, 0))] * 2
        args += list(rope_tabs)
    else:
        assert nt == 1
        in_specs += [pl.BlockSpec((1, t, KV_W), tile3)] * 2
        args += [k, v]
    return pl.pallas_call(
        functools.partial(_p2_kernel, latent, alpha, s),
        grid=(b, nt),
        in_specs=in_specs,
        out_specs=pl.BlockSpec((1, t, d), tile3),
        out_shape=jax.ShapeDtypeStruct((b, s, d), F32),
        compiler_params=pltpu.CompilerParams(
            dimension_semantics=("parallel", "parallel"),
            vmem_limit_bytes=V7X_VMEM_BYTES * 7 // 8),
        name="pass2_latent" if latent else "pass2_context",
    )(*args)


def _dft_tables(n, m):
    idx = jnp.arange(m, dtype=jnp.int32)
    r = 64
    if m <= 256 or m % r:
        ang = (2.0 * np.pi / n) * ((idx[:, None] * idx[None, :]) % n).astype(F32)
        return jnp.cos(ang), jnp.sin(ang)
    hi = jnp.arange(m // r, dtype=jnp.int32)
    lo = jnp.arange(r, dtype=jnp.int32)
    a = (2.0 * np.pi / n) * ((r * hi[:, None] * idx[None, :]) % n).astype(F32)
    bb = (2.0 * np.pi / n) * ((lo[:, None] * idx[None, :]) % n).astype(F32)
    ca, sa = jnp.cos(a)[:, None, :], jnp.sin(a)[:, None, :]
    cb, sb = jnp.cos(bb)[None, :, :], jnp.sin(bb)[None, :, :]
    return (ca * cb - sa * sb).reshape(m, m), (sa * cb + ca * sb).reshape(m, m)


def _position_tables(s):
    c, sn = _dft_tables(s, s // 2)
    scale = s ** -0.5
    return (c * scale).astype(BF16), (sn * scale).astype(BF16)


def _reversal_matrix():
    r = np.arange(REV_BLOCK)[:, None]
    c = np.arange(2 * REV_BLOCK)[None, :]
    return jnp.asarray(c == REV_BLOCK - r, BF16)


def _channel_matrix():
    c, sn = _dft_tables(FOU_GW, FOU_GW)
    return (jnp.concatenate([c, sn], axis=1) * (FOU_GW ** -0.5)).astype(BF16)


def _rope_tables(n_tokens):
    pos = jnp.arange(n_tokens, dtype=jnp.int32)
    row = (pos // GRID_W).astype(F32)
    col = (pos % GRID_W).astype(F32)
    n_freq = ROT_AXIS // 2
    inv_freq = ROPE_BASE ** (-jnp.arange(n_freq, dtype=F32) / n_freq)
    lane = np.arange(LANES)
    dd = lane % HEAD_DIM
    fidx = jnp.asarray(dd % n_freq)
    use_row = jnp.asarray(dd < ROT_AXIS)
    ang = jnp.where(use_row[None, :], row[:, None], col[:, None]) * inv_freq[fidx][None, :]
    sign = jnp.asarray(np.where((lane % ROT_AXIS) < n_freq, -1.0, 1.0), F32)
    return jnp.cos(ang), jnp.sin(ang) * sign[None, :]


def _permute_heads(w, axis):
    shp = w.shape
    w = w.reshape(shp[:axis] + (KV_HEADS, Q_PER_KV, HEAD_DIM) + shp[axis + 1:])
    return jnp.swapaxes(w, axis, axis + 1).reshape(shp)


def kernel(x_prompt, x_sample, cache_k, cache_v, c, c_ctx, w_mod, b_mod, w_in, sink, conv_w, w_branch, w_o, ln_g, ln_b):
    depth = w_in.shape[0]
    alpha = float((2 * depth) ** 0.25)
    dec_b, dec_s, d = x_sample.shape
    ctx_b, ctx_s, _ = x_prompt.shape

    rows = -(-(dec_b + 1) // SUBLANES) * SUBLANES
    cond = jnp.zeros((rows, d), F32).at[:dec_b].set(c).at[dec_b].set(c_ctx)
    mod = _modulation(cond, w_mod, b_mod).reshape(depth, rows, 3, d)

    sl = lambda a, bnd: w_in[:, :, a:bnd]
    w1 = jnp.concatenate([sl(O_K, O_V), sl(O_V, O_ZA), sl(O_CC, O_CX), sl(O_CX, O_ZB), sl(O_FX, O_ZC)],
                         axis=-1).astype(BF16)
    w2 = jnp.concatenate([_permute_heads(sl(O_Q, O_K), 2), _permute_heads(sl(O_ZA, O_CB), 2), sl(O_CB, O_CC),
                          sl(O_ZB, O_FX), sl(O_ZC, O_G), sl(O_G, IN_W)], axis=-1).astype(BF16)
    wb = jnp.concatenate([_permute_heads(w_branch[:, 0:1], 2), w_branch[:, 1:]], axis=1).astype(BF16)
    wo = w_o.astype(BF16)
    sink_b = jnp.broadcast_to(sink.reshape(depth * N_HEADS, 1), (depth * N_HEADS, LANES))
    ln_g3 = ln_g.reshape(depth, 1, d)
    ln_b3 = ln_b.reshape(depth, 1, d)

    dftc = _channel_matrix()
    rev = _reversal_matrix()
    rope_tabs = _rope_tables(dec_s)
    past = cache_k.shape[2]
    ck = cache_k.reshape(dec_b, depth, past, KV_W)
    cv = cache_v.reshape(dec_b, depth, past, KV_W)

    def run_group(x, mod_row, latent):
        b, s, _ = x.shape
        ch, sh = _position_tables(s)
        y = x
        ks, vs = [], []
        for l in range(depth):
            k, v, u, z = _pass1(y, mod, mod_row, l, w1, dftc, rope_tabs, latent)
            yf = _position_dft(ch, sh, rev, z.reshape(b, 2 * s, FOU_W))
            y = _pass2(y, mod, mod_row, l, w2, wb, wo, conv_w, ln_g3, ln_b3, sink_b, yf, u, k, v,
                       ck, cv, rope_tabs, latent, alpha)
            ks.append(k)
            vs.append(v)
        return y, ks, vs

    y_prompt, ks, vs = run_group(x_prompt, dec_b, False)
    new_k = jnp.stack(ks, axis=1).reshape(ctx_b, depth, ctx_s, KV_HEADS, HEAD_DIM)
    new_v = jnp.stack(vs, axis=1).reshape(ctx_b, depth, ctx_s, KV_HEADS, HEAD_DIM)
    y_sample, _, _ = run_group(x_sample, None, True)
    return (y_prompt, y_sample, new_k, new_v)
```

```python
import functools

import numpy as np
import jax
import jax.numpy as jnp
from jax import lax
from jax.experimental import pallas as pl
from jax.experimental.pallas import tpu as pltpu

F32 = jnp.float32
BF16 = jnp.bfloat16

D_MODEL = 1024
GRID_W = 64
N_HEADS = 8
KV_HEADS = 2
HEAD_DIM = 64
Q_PER_KV = N_HEADS // KV_HEADS
ATT_W = N_HEADS * HEAD_DIM
KV_W = KV_HEADS * HEAD_DIM
WINDOW = 128
CONV_W = 512
FOU_GROUPS = 4
FOU_GW = 128
FOU_W = FOU_GROUPS * FOU_GW
BRANCH_W = 512
N_BRANCH = 3
ROT_AXIS = HEAD_DIM // 2
ROPE_BASE = 10000.0
LN_EPS = 1e-6
NEG = -1e30
LOG2E = 1.4426950408889634

_OFF = np.cumsum([0, ATT_W, KV_W, KV_W, ATT_W, CONV_W, CONV_W, CONV_W, CONV_W, FOU_W, FOU_W, N_BRANCH * D_MODEL])
(O_Q, O_K, O_V, O_ZA, O_CB, O_CC, O_CX, O_ZB, O_FX, O_ZC, O_G, IN_W) = [int(v) for v in _OFF]

W1_COLS = 2 * KV_W + 2 * CONV_W + FOU_W
W2_COLS = 2 * ATT_W + 2 * CONV_W + FOU_W + N_BRANCH * D_MODEL
P2_Q, P2_ZA, P2_CB, P2_ZB, P2_ZC, P2_G = 0, 512, 1024, 1536, 2048, 2560

V7X_VMEM_BYTES = 64 * 1024 * 1024
LANES = 128
SUBLANES = 8

P1_TOKEN_TILE = 1024
SUB_TILE = 256
P2_SUB_TILES = 2
REV_BLOCK = 128
MOD_COL_TILE = 768


def _layernorm(x):
    mu = jnp.mean(x, axis=-1, keepdims=True)
    xc = x - mu
    var = jnp.mean(xc * xc, axis=-1, keepdims=True)
    return xc * lax.rsqrt(var + LN_EPS)


def _silu(z):
    return z * jax.nn.sigmoid(z)


def _rope(x, cos, sin_signed):
    lane = lax.broadcasted_iota(jnp.int32, x.shape, 1)
    partner = jnp.where((lane & 16) == 0, pltpu.roll(x, LANES - 16, 1), pltpu.roll(x, 16, 1))
    return x * cos + partner * sin_signed


def _mod_kernel(c_ref, w_ref, b_ref, o_ref):
    c = c_ref[...]
    a = _silu(c).astype(BF16)
    o_ref[0] = jnp.dot(a, w_ref[0].astype(BF16), preferred_element_type=F32) + b_ref[0]


def _modulation(cond, w_mod, b_mod):
    depth, d, n = w_mod.shape
    rows = cond.shape[0]
    return pl.pallas_call(
        _mod_kernel,
        grid=(depth, n // MOD_COL_TILE),
        in_specs=[
            pl.BlockSpec((rows, d), lambda l, j: (0, 0)),
            pl.BlockSpec((1, d, MOD_COL_TILE), lambda l, j: (l, 0, j)),
            pl.BlockSpec((1, 1, MOD_COL_TILE), lambda l, j: (l, 0, j)),
        ],
        out_specs=pl.BlockSpec((1, rows, MOD_COL_TILE), lambda l, j: (l, 0, j)),
        out_shape=jax.ShapeDtypeStruct((depth, rows, n), F32),
        compiler_params=pltpu.CompilerParams(dimension_semantics=("parallel", "parallel")),
        name="modulation",
    )(cond, w_mod, b_mod.reshape(depth, 1, n))


def _p1_kernel(latent, *refs):
    if latent:
        x_ref, mod_ref, w1_ref, dftc_ref, cos_ref, sin_ref, h_ref, k_ref, v_ref, u_ref, z_ref = refs
    else:
        x_ref, mod_ref, w1_ref, dftc_ref, h_ref, k_ref, v_ref, u_ref, z_ref = refs
    x = x_ref[0]
    shift = mod_ref[0, 0, 0:1, :]
    scale = mod_ref[0, 0, 1:2, :]
    hb = (_layernorm(x) * (1.0 + scale) + shift).astype(BF16)
    h_ref[0] = hb
    p = jnp.dot(hb, w1_ref[0], preferred_element_type=F32)
    k = p[:, 0:KV_W]
    if latent:
        k = _rope(k, cos_ref[...], sin_ref[...])
    k_ref[0] = k
    v_ref[0] = p[:, KV_W:2 * KV_W]
    o = 2 * KV_W
    u_ref[0] = p[:, o:o + CONV_W] * p[:, o + CONV_W:o + 2 * CONV_W]
    o += 2 * CONV_W
    for g in range(FOU_GROUPS):
        fg = p[:, o + g * FOU_GW:o + (g + 1) * FOU_GW].astype(BF16)
        zg = jnp.dot(fg, dftc_ref[...], preferred_element_type=F32)
        z_ref[0, 0, :, g * FOU_GW:(g + 1) * FOU_GW] = zg[:, :FOU_GW].astype(BF16)
        z_ref[0, 1, :, g * FOU_GW:(g + 1) * FOU_GW] = zg[:, FOU_GW:].astype(BF16)


def _pass1(x, mod, mod_row, layer, w1, dftc, rope_tabs, latent):
    b, s, d = x.shape
    t = min(P1_TOKEN_TILE, s)
    assert s % t == 0
    nt = s // t
    if mod_row is None:
        mod_idx = lambda i, j: (layer, i, 0, 0)
    else:
        mod_idx = lambda i, j: (layer, mod_row, 0, 0)
    in_specs = [
        pl.BlockSpec((1, t, d), lambda i, j: (i, j, 0)),
        pl.BlockSpec((1, 1, 3, d), mod_idx),
        pl.BlockSpec((1, d, W1_COLS), lambda i, j: (layer, 0, 0)),
        pl.BlockSpec((FOU_GW, 2 * FOU_GW), lambda i, j: (0, 0)),
    ]
    args = [x, mod, w1, dftc]
    if latent:
        in_specs += [pl.BlockSpec((t, KV_W), lambda i, j: (j, 0))] * 2
        args += list(rope_tabs)
    out_shape = [
        jax.ShapeDtypeStruct((b, s, d), BF16),
        jax.ShapeDtypeStruct((b, s, KV_W), F32),
        jax.ShapeDtypeStruct((b, s, KV_W), F32),
        jax.ShapeDtypeStruct((b, s, CONV_W), F32),
        jax.ShapeDtypeStruct((b, 2, s, FOU_W), BF16),
    ]
    out_specs = [
        pl.BlockSpec((1, t, d), lambda i, j: (i, j, 0)),
        pl.BlockSpec((1, t, KV_W), lambda i, j: (i, j, 0)),
        pl.BlockSpec((1, t, KV_W), lambda i, j: (i, j, 0)),
        pl.BlockSpec((1, t, CONV_W), lambda i, j: (i, j, 0)),
        pl.BlockSpec((1, 2, t, FOU_W), lambda i, j: (i, 0, j, 0)),
    ]
    return pl.pallas_call(
        functools.partial(_p1_kernel, latent),
        grid=(b, nt),
        in_specs=in_specs,
        out_specs=out_specs,
        out_shape=out_shape,
        compiler_params=pltpu.CompilerParams(
            dimension_semantics=("parallel", "parallel"),
            vmem_limit_bytes=V7X_VMEM_BYTES * 3 // 4),
        name="pass1_latent" if latent else "pass1_context",
    )(*args)


def _dft_kernel(n, tm, z_ref, ch_ref, sh_ref, rev_ref, o_ref, e_ref, od_ref, ab_ref, zh_ref):
    h = n // 2
    blk = rev_ref.shape[0]
    nblk = n // blk
    mi = pl.program_id(1)
    inv = n ** -0.5
    rev = rev_ref[...]

    @pl.when(mi == 0)
    def _fold():
        r = lax.broadcasted_iota(jnp.int32, (blk, 1), 0)
        sgn = jnp.where((r & 1) == 0, 1.0, -1.0)
        asum = jnp.zeros((1, e_ref.shape[1]), F32)
        for i in range(h // blk):
            q = nblk - 1 - i
            if i == 0:
                pad = jnp.zeros((blk, e_ref.shape[1]), BF16)
                mir_r = jnp.concatenate([z_ref[0, q * blk:(q + 1) * blk, :], pad], axis=0)
                mir_i = jnp.concatenate([z_ref[0, n + q * blk:n + (q + 1) * blk, :], pad], axis=0)
            else:
                mir_r = z_ref[0, q * blk:(q + 2) * blk, :]
                mir_i = z_ref[0, n + q * blk:n + (q + 2) * blk, :]
            e = z_ref[0, i * blk:(i + 1) * blk, :].astype(F32) + jnp.dot(rev, mir_r, preferred_element_type=F32)
            o = (z_ref[0, n + i * blk:n + (i + 1) * blk, :].astype(F32)
                 - jnp.dot(rev, mir_i, preferred_element_type=F32))
            e_ref[i * blk:(i + 1) * blk, :] = e.astype(BF16)
            od_ref[i * blk:(i + 1) * blk, :] = o.astype(BF16)
            asum = asum + jnp.sum(e * sgn, axis=0, keepdims=True)
        zh = z_ref[0, h:h + 1, :].astype(F32) * inv
        zh_ref[...] = zh
        ab_ref[h:h + blk, :] = jnp.where(r == 0, asum * inv + zh, 0.0).astype(BF16)

    a = jnp.dot(ch_ref[...], e_ref[...], preferred_element_type=F32)
    b = jnp.dot(sh_ref[...], od_ref[...], preferred_element_type=F32)
    rr = lax.broadcasted_iota(jnp.int32, (tm, 1), 0)
    a = a + jnp.where((rr & 1) == 0, 1.0, -1.0) * zh_ref[...]
    rows = pl.ds(pl.multiple_of(mi * tm, tm), tm)
    o_ref[0, rows, :] = (a - b).astype(o_ref.dtype)
    ab_ref[rows, :] = (a + b).astype(BF16)

    @pl.when(mi == pl.num_programs(1) - 1)
    def _unfold():
        for p in range(h // blk):
            q = h // blk - p - 1
            win = ab_ref[q * blk:(q + 2) * blk, :]
            o_ref[0, h + p * blk:h + (p + 1) * blk, :] = jnp.dot(
                rev, win, preferred_element_type=F32).astype(o_ref.dtype)


def _position_dft(ch, sh, rev, z):
    b, s2, w = z.shape
    s = s2 // 2
    h = s // 2
    blk = rev.shape[0]
    assert h % blk == 0 and rev.shape[1] == 2 * blk
    tm = min(512, h)
    return pl.pallas_call(
        functools.partial(_dft_kernel, s, tm),
        grid=(b, h // tm),
        in_specs=[
            pl.BlockSpec((1, s2, w), lambda i, m: (i, 0, 0)),
            pl.BlockSpec((tm, h), lambda i, m: (m, 0)),
            pl.BlockSpec((tm, h), lambda i, m: (m, 0)),
            pl.BlockSpec((blk, 2 * blk), lambda i, m: (0, 0)),
        ],
        out_specs=pl.BlockSpec((1, s, w), lambda i, m: (i, 0, 0)),
        out_shape=jax.ShapeDtypeStruct((b, s, w), BF16),
        scratch_shapes=[pltpu.VMEM((h, w), BF16), pltpu.VMEM((h, w), BF16),
                        pltpu.VMEM((h + blk, w), BF16), pltpu.VMEM((1, w), F32)],
        compiler_params=pltpu.CompilerParams(
            dimension_semantics=("parallel", "arbitrary"),
            vmem_limit_bytes=V7X_VMEM_BYTES * 3 // 4),
        name="position_dft",
    )(z, ch, sh, rev)


def _scores(qpad, keys, masks):
    scores = []
    for kb, mk in zip(keys, masks):
        s = lax.dot_general(qpad, kb, (((1,), (1,)), ((), ())), preferred_element_type=F32)
        if mk is not None:
            s = jnp.where(mk, s, NEG)
        scores.append(s)
    return scores


def _softmax_pv(scores, vals_aug, sink2):
    m = sink2
    for s in scores:
        m = jnp.maximum(m, jnp.max(s, axis=-1, keepdims=True))
    acc = None
    for s, vb in zip(scores, vals_aug):
        pv = jnp.dot(jnp.exp2(s - m).astype(BF16), vb, preferred_element_type=F32)
        acc = pv if acc is None else acc + pv
    return acc, jnp.exp2(sink2 - m)


def _p2_kernel(latent, alpha, seq_len, nsub, *refs):
    if latent:
        (x_ref, h_ref, mod_ref, w2_ref, wb_ref, wo_ref, cw_ref, lng_ref, lnb_ref, sink_ref, yf_ref,
         um_ref, up_ref, un_ref, km_ref, kp_ref, kn_ref, vm_ref, vp_ref, vn_ref,
         kc_ref, vc_ref, cos_ref, sin_ref, y_ref) = refs
    else:
        (x_ref, h_ref, mod_ref, w2_ref, wb_ref, wo_ref, cw_ref, lng_ref, lnb_ref, sink_ref, yf_ref,
         um_ref, km_ref, vm_ref, y_ref) = refs
    t = pl.program_id(1)
    nt = pl.num_programs(1)
    tq = SUB_TILE
    tile = nsub * tq if latent else tq

    def window(main_ref, prev_ref, next_ref, bi, lo, hi):
        parts = []
        if lo < 0:
            parts.append(prev_ref[bi])
            lo = 0
        tail = hi > tile
        parts.append(main_ref[bi, lo:min(hi, tile), :])
        if tail:
            parts.append(next_ref[bi])
        return parts[0] if len(parts) == 1 else jnp.concatenate(parts, axis=0)

    def chain(sub):
        bi, r0 = (0, sub * tq) if latent else (sub, 0)
        rows = slice(r0, r0 + tq)
        hb = h_ref[bi, rows, :]

        def proj(off, width):
            return jnp.dot(hb, w2_ref[0, :, off:off + width], preferred_element_type=F32)

        q = proj(P2_Q, ATT_W)
        if latent:
            keys = [window(km_ref, kp_ref, kn_ref, 0, r0 - WINDOW, r0 + tq + WINDOW).astype(BF16),
                    kc_ref[0, 0].astype(BF16)]
            vals = [window(vm_ref, vp_ref, vn_ref, 0, r0 - WINDOW, r0 + tq + WINDOW).astype(BF16),
                    vc_ref[0, 0].astype(BF16)]
            nloc = keys[0].shape[0]
            r = lax.broadcasted_iota(jnp.int32, (tq, nloc), 0)
            j = lax.broadcasted_iota(jnp.int32, (tq, nloc), 1)
            dlt = j - r
            kpos = j + (t * tile + r0 - WINDOW)
            band = (dlt >= 0) & (dlt <= 2 * WINDOW) & (kpos >= 0) & (kpos < seq_len)
            masks = [band, None]
            cos = cos_ref[rows, :]
            sin = sin_ref[rows, :]
        else:
            keys = [km_ref[bi].astype(BF16)]
            vals = [vm_ref[bi].astype(BF16)]
            masks = [None]
        lane = lax.broadcasted_iota(jnp.int32, (tq, LANES), 1)
        low = lane < HEAD_DIM
        one = jnp.ones((), BF16)
        vals_lo, vals_hi = [], []
        for vb in vals:
            low_k = lax.broadcasted_iota(jnp.int32, vb.shape, 1) < HEAD_DIM
            vals_lo.append(jnp.where(low_k, vb, one))
            vals_hi.append(jnp.where(low_k, one, vb))
        heads = []
        for g in range(Q_PER_KV):
            qc = q[:, g * LANES:(g + 1) * LANES]
            if latent:
                qc = _rope(qc, cos, sin)
            qc = qc * (HEAD_DIM ** -0.5 * LOG2E)
            heads.append((jnp.where(low, qc, 0.0).astype(BF16), vals_lo, sink_ref[g:g + 1, 0:1] * LOG2E))
            heads.append((jnp.where(low, 0.0, qc).astype(BF16), vals_hi,
                          sink_ref[Q_PER_KV + g:Q_PER_KV + g + 1, 0:1] * LOG2E))
        side_cols = [(P2_ZA, ATT_W), (P2_CB, CONV_W), (P2_ZB, CONV_W), (P2_ZC, FOU_W)]
        side_cols += [(P2_G + br * D_MODEL, D_MODEL) for br in range(N_BRANCH)]
        side = []
        results = []
        sc = _scores(heads[0][0], keys, masks)
        for i, (_, vaug, sink2) in enumerate(heads):
            nxt = _scores(heads[i + 1][0], keys, masks) if i + 1 < len(heads) else None
            if i < len(side_cols):
                side.append(proj(*side_cols[i]))
            results.append(_softmax_pv(sc, vaug, sink2))
            sc = nxt
        p_za, p_cb, p_zb, p_zc = side[:4]
        p_gates = side[4:]
        ya_chunks = []
        for g in range(Q_PER_KV):
            (a_lo, t_lo), (a_hi, t_hi) = results[2 * g], results[2 * g + 1]
            num = jnp.where(low, a_lo, a_hi)
            den = pltpu.roll(jnp.where(low, a_hi, a_lo), HEAD_DIM, 1) + jnp.where(low, t_lo, t_hi)
            ya_chunks.append(num / den)
        ya = jnp.concatenate(ya_chunks, axis=-1)
        y_a = ya * _silu(p_za)

        yield

        u = um_ref[bi, rows, :]
        if not latent:
            prev_row = next_row = jnp.zeros((1, CONV_W), F32)
        else:
            if r0 == 0:
                prev_row = jnp.where(t > 0, up_ref[0, SUBLANES - 1:SUBLANES, :], 0.0)
            else:
                prev_row = um_ref[0, r0 - 1:r0, :]
            if r0 + tq == tile:
                next_row = jnp.where(t < nt - 1, un_ref[0, 0:1, :], 0.0)
            else:
                next_row = um_ref[0, r0 + tq:r0 + tq + 1, :]
        row = lax.broadcasted_iota(jnp.int32, u.shape, 0)
        u_up = jnp.where(row == 0, prev_row, pltpu.roll(u, 1, 0))
        u_dn = jnp.where(row == tq - 1, next_row, pltpu.roll(u, tq - 1, 0))
        conv = u_up * cw_ref[0, 0:1, :] + u * cw_ref[0, 1:2, :] + u_dn * cw_ref[0, 2:3, :]
        y_b = p_cb * conv * _silu(p_zb)

        y_c = yf_ref[bi, rows, :].astype(F32) * _silu(p_zc)

        merged = None
        for br, yb in enumerate((y_a, y_b, y_c)):
            gt = jax.nn.sigmoid(p_gates[br])
            term = gt * jnp.dot(yb.astype(BF16), wb_ref[0, br], preferred_element_type=F32)
            merged = term if merged is None else merged + term
        out = jnp.dot(merged.astype(BF16), wo_ref[0], preferred_element_type=F32)

        yield

        res = alpha * x_ref[bi, rows, :] + mod_ref[0, 0, 2:3, :] * out
        y_ref[bi, rows, :] = _layernorm(res) * lng_ref[0] + lnb_ref[0]
        yield

    chains = [chain(sub) for sub in range(nsub)]
    for _stage in range(3):
        for ch in chains:
            next(ch)


def _pass2(x, hb, mod, mod_row, layer, w2, wb, wo, conv_w, ln_g, ln_b, sink_b, yf, u, k, v,
           cache_k, cache_v, rope_tabs, latent, alpha):
    b, s, d = x.shape
    nsub = P2_SUB_TILES
    if latent:
        bb, t = 1, nsub * SUB_TILE
        assert s % t == 0 and SUB_TILE == 2 * WINDOW
    else:
        bb, t = nsub, s
        assert s == SUB_TILE and b % nsub == 0
    nt = s // t
    if mod_row is None:
        mod_idx = lambda i, j: (layer, i, 0, 0)
    else:
        mod_idx = lambda i, j: (layer, mod_row, 0, 0)
    const3 = lambda i, j: (layer, 0, 0)
    tile3 = lambda i, j: (i, j, 0)
    in_specs = [
        pl.BlockSpec((bb, t, d), tile3),
        pl.BlockSpec((bb, t, d), tile3),
        pl.BlockSpec((1, 1, 3, d), mod_idx),
        pl.BlockSpec((1, d, W2_COLS), const3),
        pl.BlockSpec((1, N_BRANCH, BRANCH_W, d), lambda i, j: (layer, 0, 0, 0)),
        pl.BlockSpec((1, d, d), const3),
        pl.BlockSpec((1, 3, CONV_W), const3),
        pl.BlockSpec((1, 1, d), const3),
        pl.BlockSpec((1, 1, d), const3),
        pl.BlockSpec((N_HEADS, LANES), lambda i, j: (layer, 0)),
        pl.BlockSpec((bb, t, FOU_W), tile3),
        pl.BlockSpec((bb, t, CONV_W), tile3),
    ]
    args = [x, hb, mod, w2, wb, wo, conv_w, ln_g, ln_b, sink_b, yf, u]
    if latent:
        rpt = t // SUBLANES
        wpt = t // WINDOW
        nb8 = s // SUBLANES
        nbw = s // WINDOW
        prev8 = lambda i, j: (i, jnp.maximum(j * rpt - 1, 0), 0)
        next8 = lambda i, j: (i, jnp.minimum((j + 1) * rpt, nb8 - 1), 0)
        prevw = lambda i, j: (i, jnp.maximum(j * wpt - 1, 0), 0)
        nextw = lambda i, j: (i, jnp.minimum((j + 1) * wpt, nbw - 1), 0)
        in_specs += [pl.BlockSpec((1, SUBLANES, CONV_W), prev8), pl.BlockSpec((1, SUBLANES, CONV_W), next8)]
        args += [u, u]
        for arr in (k, v):
            in_specs += [pl.BlockSpec((1, t, KV_W), tile3),
                         pl.BlockSpec((1, WINDOW, KV_W), prevw),
                         pl.BlockSpec((1, WINDOW, KV_W), nextw)]
            args += [arr, arr, arr]
        past = cache_k.shape[2]
        cache_spec = pl.BlockSpec((1, 1, past, KV_W), lambda i, j: (i, layer, 0, 0))
        in_specs += [cache_spec, cache_spec]
        args += [cache_k, cache_v]
        in_specs += [pl.BlockSpec((t, KV_W), lambda i, j: (j, 0))] * 2
        args += list(rope_tabs)
    else:
        in_specs += [pl.BlockSpec((bb, t, KV_W), tile3)] * 2
        args += [k, v]
    return pl.pallas_call(
        functools.partial(_p2_kernel, latent, alpha, s, nsub),
        grid=(b // bb, nt),
        in_specs=in_specs,
        out_specs=pl.BlockSpec((bb, t, d), tile3),
        out_shape=jax.ShapeDtypeStruct((b, s, d), F32),
        compiler_params=pltpu.CompilerParams(
            dimension_semantics=("parallel", "parallel"),
            vmem_limit_bytes=V7X_VMEM_BYTES * 7 // 8),
        name="pass2_latent" if latent else "pass2_context",
    )(*args)


def _dft_tables(n, m):
    idx = jnp.arange(m, dtype=jnp.int32)
    r = 64
    if m <= 256 or m % r:
        ang = (2.0 * np.pi / n) * ((idx[:, None] * idx[None, :]) % n).astype(F32)
        return jnp.cos(ang), jnp.sin(ang)
    hi = jnp.arange(m // r, dtype=jnp.int32)
    lo = jnp.arange(r, dtype=jnp.int32)
    a = (2.0 * np.pi / n) * ((r * hi[:, None] * idx[None, :]) % n).astype(F32)
    bb = (2.0 * np.pi / n) * ((lo[:, None] * idx[None, :]) % n).astype(F32)
    ca, sa = jnp.cos(a)[:, None, :], jnp.sin(a)[:, None, :]
    cb, sb = jnp.cos(bb)[None, :, :], jnp.sin(bb)[None, :, :]
    return (ca * cb - sa * sb).reshape(m, m), (sa * cb + ca * sb).reshape(m, m)


def _position_tables(s):
    c, sn = _dft_tables(s, s // 2)
    scale = s ** -0.5
    return (c * scale).astype(BF16), (sn * scale).astype(BF16)


def _reversal_matrix():
    r = np.arange(REV_BLOCK)[:, None]
    c = np.arange(2 * REV_BLOCK)[None, :]
    return jnp.asarray(c == REV_BLOCK - r, BF16)


def _channel_matrix():
    idx = np.arange(FOU_GW)
    ang = (2.0 * np.pi / FOU_GW) * ((idx[:, None] * idx[None, :]) % FOU_GW)
    tab = np.concatenate([np.cos(ang), np.sin(ang)], axis=1) * (FOU_GW ** -0.5)
    return jnp.asarray(tab, F32).astype(BF16)


def _rope_tables(n_tokens):
    pos = np.arange(n_tokens)
    row = (pos // GRID_W).astype(np.float64)
    col = (pos % GRID_W).astype(np.float64)
    n_freq = ROT_AXIS // 2
    inv_freq = ROPE_BASE ** (-np.arange(n_freq, dtype=np.float64) / n_freq)
    lane = np.arange(LANES)
    dd = lane % HEAD_DIM
    ang = np.where((dd < ROT_AXIS)[None, :], row[:, None], col[:, None]) * inv_freq[dd % n_freq][None, :]
    sign = np.where((lane % ROT_AXIS) < n_freq, -1.0, 1.0)
    return jnp.asarray(np.cos(ang), F32), jnp.asarray(np.sin(ang) * sign[None, :], F32)


def _permute_heads(w, axis):
    shp = w.shape
    w = w.reshape(shp[:axis] + (KV_HEADS, Q_PER_KV, HEAD_DIM) + shp[axis + 1:])
    return jnp.swapaxes(w, axis, axis + 1).reshape(shp)


def kernel(x_prompt, x_sample, cache_k, cache_v, c, c_ctx, w_mod, b_mod, w_in, sink, conv_w, w_branch, w_o, ln_g, ln_b):
    depth = w_in.shape[0]
    alpha = float((2 * depth) ** 0.25)
    dec_b, dec_s, d = x_sample.shape
    ctx_b, ctx_s, _ = x_prompt.shape

    rows = -(-(dec_b + 1) // SUBLANES) * SUBLANES
    cond = jnp.concatenate([c, c_ctx[None, :], jnp.zeros((rows - dec_b - 1, d), F32)], axis=0)
    mod = _modulation(cond, w_mod, b_mod).reshape(depth, rows, 3, d)

    sl = lambda a, bnd: w_in[:, :, a:bnd]
    w1 = jnp.concatenate([sl(O_K, O_V), sl(O_V, O_ZA), sl(O_CC, O_CX), sl(O_CX, O_ZB), sl(O_FX, O_ZC)],
                         axis=-1).astype(BF16)
    w2 = jnp.concatenate([_permute_heads(sl(O_Q, O_K), 2), _permute_heads(sl(O_ZA, O_CB), 2), sl(O_CB, O_CC),
                          sl(O_ZB, O_FX), sl(O_ZC, O_G), sl(O_G, IN_W)], axis=-1).astype(BF16)
    wb = jnp.concatenate([_permute_heads(w_branch[:, 0:1], 2), w_branch[:, 1:]], axis=1).astype(BF16)
    wo = w_o.astype(BF16)
    sink_b = jnp.broadcast_to(sink.reshape(depth * N_HEADS, 1), (depth * N_HEADS, LANES))
    ln_g3 = ln_g.reshape(depth, 1, d)
    ln_b3 = ln_b.reshape(depth, 1, d)

    dftc = _channel_matrix()
    rev = _reversal_matrix()
    rope_tabs = _rope_tables(dec_s)
    past = cache_k.shape[2]
    ck = cache_k.reshape(dec_b, depth, past, KV_W)
    cv = cache_v.reshape(dec_b, depth, past, KV_W)

    def run_group(x, mod_row, latent):
        b, s, _ = x.shape
        ch, sh = _position_tables(s)
        y = x
        ks, vs = [], []
        for l in range(depth):
            hb, k, v, u, z = _pass1(y, mod, mod_row, l, w1, dftc, rope_tabs, latent)
            yf = _position_dft(ch, sh, rev, z.reshape(b, 2 * s, FOU_W))
            y = _pass2(y, hb, mod, mod_row, l, w2, wb, wo, conv_w, ln_g3, ln_b3, sink_b, yf, u, k, v,
                       ck, cv, rope_tabs, latent, alpha)
            ks.append(k)
            vs.append(v)
        return y, ks, vs

    y_prompt, ks, vs = run_group(x_prompt, dec_b, False)
    new_k = jnp.stack(ks, axis=1).reshape(ctx_b, depth, ctx_s, KV_HEADS, HEAD_DIM)
    new_v = jnp.stack(vs, axis=1).reshape(ctx_b, depth, ctx_s, KV_HEADS, HEAD_DIM)
    y_sample, _, _ = run_group(x_sample, None, True)
    return (y_prompt, y_sample, new_k, new_v)
```

```python
import functools

import numpy as np
import jax
import jax.numpy as jnp
from jax import lax
from jax.experimental import pallas as pl
from jax.experimental.pallas import tpu as pltpu

F32 = jnp.float32
BF16 = jnp.bfloat16

D_MODEL = 1024
GRID_W = 64
N_HEADS = 8
KV_HEADS = 2
HEAD_DIM = 64
Q_PER_KV = N_HEADS // KV_HEADS
ATT_W = N_HEADS * HEAD_DIM
KV_W = KV_HEADS * HEAD_DIM
WINDOW = 128
CONV_W = 512
FOU_GROUPS = 4
FOU_GW = 128
FOU_W = FOU_GROUPS * FOU_GW
BRANCH_W = 512
N_BRANCH = 3
ROT_AXIS = HEAD_DIM // 2
ROPE_BASE = 10000.0
LN_EPS = 1e-6
NEG = -1e30
LOG2E = 1.4426950408889634

_OFF = np.cumsum([0, ATT_W, KV_W, KV_W, ATT_W, CONV_W, CONV_W, CONV_W, CONV_W, FOU_W, FOU_W, N_BRANCH * D_MODEL])
(O_Q, O_K, O_V, O_ZA, O_CB, O_CC, O_CX, O_ZB, O_FX, O_ZC, O_G, IN_W) = [int(v) for v in _OFF]

W1_COLS = 2 * KV_W + 2 * CONV_W + FOU_W
W2_COLS = 2 * ATT_W + 2 * CONV_W + FOU_W + N_BRANCH * D_MODEL
P2_Q, P2_ZA, P2_CB, P2_ZB, P2_ZC, P2_G = 0, 512, 1024, 1536, 2048, 2560

V7X_VMEM_BYTES = 64 * 1024 * 1024
LANES = 128
SUBLANES = 8

P1_TOKEN_TILE = 1024
SUB_TILE = 256
P2_SUB_TILES = 2
REV_BLOCK = 128
MOD_COL_TILE = 768


def _layernorm(x):
    mu = jnp.mean(x, axis=-1, keepdims=True)
    xc = x - mu
    var = jnp.mean(xc * xc, axis=-1, keepdims=True)
    return xc * lax.rsqrt(var + LN_EPS)


def _silu(z):
    return z * jax.nn.sigmoid(z)


def _rope(x, cos, sin_signed):
    lane = lax.broadcasted_iota(jnp.int32, x.shape, 1)
    partner = jnp.where((lane & 16) == 0, pltpu.roll(x, LANES - 16, 1), pltpu.roll(x, 16, 1))
    return x * cos + partner * sin_signed


def _mod_kernel(c_ref, w_ref, b_ref, o_ref):
    c = c_ref[...]
    a = _silu(c).astype(BF16)
    o_ref[0] = jnp.dot(a, w_ref[0].astype(BF16), preferred_element_type=F32) + b_ref[0]


def _modulation(cond, w_mod, b_mod):
    depth, d, n = w_mod.shape
    rows = cond.shape[0]
    return pl.pallas_call(
        _mod_kernel,
        grid=(depth, n // MOD_COL_TILE),
        in_specs=[
            pl.BlockSpec((rows, d), lambda l, j: (0, 0)),
            pl.BlockSpec((1, d, MOD_COL_TILE), lambda l, j: (l, 0, j)),
            pl.BlockSpec((1, 1, MOD_COL_TILE), lambda l, j: (l, 0, j)),
        ],
        out_specs=pl.BlockSpec((1, rows, MOD_COL_TILE), lambda l, j: (l, 0, j)),
        out_shape=jax.ShapeDtypeStruct((depth, rows, n), F32),
        compiler_params=pltpu.CompilerParams(dimension_semantics=("parallel", "parallel")),
        name="modulation",
    )(cond, w_mod, b_mod.reshape(depth, 1, n))


def _p1_kernel(latent, *refs):
    if latent:
        x_ref, mod_ref, w1_ref, dftc_ref, cos_ref, sin_ref, h_ref, kb_ref, vt_ref, u_ref, z_ref = refs
    else:
        x_ref, mod_ref, w1_ref, dftc_ref, h_ref, kb_ref, vt_ref, u_ref, z_ref, k_ref, v_ref = refs
    shift = mod_ref[0, 0, 0:1, :]
    scale = mod_ref[0, 0, 1:2, :]
    bb, t = x_ref.shape[:2]
    if bb > 1:
        units = [(i, slice(0, t)) for i in range(bb)]
    else:
        units = [(0, slice(0, t // 2)), (0, slice(t // 2, t))]

    def chain(bi, rows):
        hb = (_layernorm(x_ref[bi, rows, :]) * (1.0 + scale) + shift).astype(BF16)
        h_ref[bi, rows, :] = hb
        yield
        p = jnp.dot(hb, w1_ref[0], preferred_element_type=F32)
        k = p[:, 0:KV_W]
        v = p[:, KV_W:2 * KV_W]
        if latent:
            k = _rope(k, cos_ref[rows, :], sin_ref[rows, :])
        else:
            k_ref[bi, rows, :] = k
            v_ref[bi, rows, :] = v
        kb_ref[bi, rows, :] = k.astype(BF16)
        vt_ref[bi, :, rows] = v.T.astype(BF16)
        o = 2 * KV_W
        u_ref[bi, rows, :] = p[:, o:o + CONV_W] * p[:, o + CONV_W:o + 2 * CONV_W]
        o += 2 * CONV_W
        for g in range(FOU_GROUPS):
            fg = p[:, o + g * FOU_GW:o + (g + 1) * FOU_GW].astype(BF16)
            zg = jnp.dot(fg, dftc_ref[...], preferred_element_type=F32)
            z_ref[bi, 0, rows, g * FOU_GW:(g + 1) * FOU_GW] = zg[:, :FOU_GW].astype(BF16)
            z_ref[bi, 1, rows, g * FOU_GW:(g + 1) * FOU_GW] = zg[:, FOU_GW:].astype(BF16)
        yield

    chains = [chain(bi, rows) for bi, rows in units]
    for _stage in range(2):
        for ch in chains:
            next(ch)


def _pass1(x, mod, mod_row, layer, w1, dftc, rope_tabs, latent):
    b, s, d = x.shape
    t = min(P1_TOKEN_TILE, s)
    bb = P1_TOKEN_TILE // t if mod_row is not None else 1
    assert s % t == 0 and b % bb == 0 and (bb == 1 or t == s)
    nt = s // t
    if mod_row is None:
        mod_idx = lambda i, j: (layer, i, 0, 0)
    else:
        mod_idx = lambda i, j: (layer, mod_row, 0, 0)
    in_specs = [
        pl.BlockSpec((bb, t, d), lambda i, j: (i, j, 0)),
        pl.BlockSpec((1, 1, 3, d), mod_idx),
        pl.BlockSpec((1, d, W1_COLS), lambda i, j: (layer, 0, 0)),
        pl.BlockSpec((FOU_GW, 2 * FOU_GW), lambda i, j: (0, 0)),
    ]
    args = [x, mod, w1, dftc]
    if latent:
        in_specs += [pl.BlockSpec((t, KV_W), lambda i, j: (j, 0))] * 2
        args += list(rope_tabs)
    out_shape = [
        jax.ShapeDtypeStruct((b, s, d), BF16),
        jax.ShapeDtypeStruct((b, s, KV_W), BF16),
        jax.ShapeDtypeStruct((b, KV_W, s), BF16),
        jax.ShapeDtypeStruct((b, s, CONV_W), F32),
        jax.ShapeDtypeStruct((b, 2, s, FOU_W), BF16),
    ]
    out_specs = [
        pl.BlockSpec((bb, t, d), lambda i, j: (i, j, 0)),
        pl.BlockSpec((bb, t, KV_W), lambda i, j: (i, j, 0)),
        pl.BlockSpec((bb, KV_W, t), lambda i, j: (i, 0, j)),
        pl.BlockSpec((bb, t, CONV_W), lambda i, j: (i, j, 0)),
        pl.BlockSpec((bb, 2, t, FOU_W), lambda i, j: (i, 0, j, 0)),
    ]
    if not latent:
        out_shape += [jax.ShapeDtypeStruct((b, s, KV_W), F32)] * 2
        out_specs += [pl.BlockSpec((bb, t, KV_W), lambda i, j: (i, j, 0))] * 2
    return pl.pallas_call(
        functools.partial(_p1_kernel, latent),
        grid=(b // bb, nt),
        in_specs=in_specs,
        out_specs=out_specs,
        out_shape=out_shape,
        compiler_params=pltpu.CompilerParams(
            dimension_semantics=("parallel", "parallel"),
            vmem_limit_bytes=V7X_VMEM_BYTES * 3 // 4),
        name="pass1_latent" if latent else "pass1_context",
    )(*args)


def _cache_kernel(k_ref, v_ref, kb_ref, vt_ref):
    kb_ref[0, 0] = k_ref[0, 0].astype(BF16)
    vt_ref[0, 0] = v_ref[0, 0].T.astype(BF16)


def _prepare_cache(cache_k, cache_v):
    b, depth, past, w = cache_k.shape
    spec = pl.BlockSpec((1, 1, past, w), lambda i, l: (i, l, 0, 0))
    return pl.pallas_call(
        _cache_kernel,
        grid=(b, depth),
        in_specs=[spec, spec],
        out_specs=[spec, pl.BlockSpec((1, 1, w, past), lambda i, l: (i, l, 0, 0))],
        out_shape=[jax.ShapeDtypeStruct((b, depth, past, w), BF16),
                   jax.ShapeDtypeStruct((b, depth, w, past), BF16)],
        compiler_params=pltpu.CompilerParams(dimension_semantics=("parallel", "parallel")),
        name="cache_prep",
    )(cache_k, cache_v)


def _dft_kernel(n, tm, z_ref, ch_ref, sh_ref, rev_ref, o_ref, e_ref, od_ref, ab_ref, zh_ref):
    h = n // 2
    blk = rev_ref.shape[0]
    nblk = n // blk
    mi = pl.program_id(1)
    inv = n ** -0.5
    rev = rev_ref[...]

    @pl.when(mi == 0)
    def _fold():
        r = lax.broadcasted_iota(jnp.int32, (blk, 1), 0)
        sgn = jnp.where((r & 1) == 0, 1.0, -1.0)
        asum = jnp.zeros((1, e_ref.shape[1]), F32)
        for i in range(h // blk):
            q = nblk - 1 - i
            if i == 0:
                pad = jnp.zeros((blk, e_ref.shape[1]), BF16)
                mir_r = jnp.concatenate([z_ref[0, q * blk:(q + 1) * blk, :], pad], axis=0)
                mir_i = jnp.concatenate([z_ref[0, n + q * blk:n + (q + 1) * blk, :], pad], axis=0)
            else:
                mir_r = z_ref[0, q * blk:(q + 2) * blk, :]
                mir_i = z_ref[0, n + q * blk:n + (q + 2) * blk, :]
            e = z_ref[0, i * blk:(i + 1) * blk, :].astype(F32) + jnp.dot(rev, mir_r, preferred_element_type=F32)
            o = (z_ref[0, n + i * blk:n + (i + 1) * blk, :].astype(F32)
                 - jnp.dot(rev, mir_i, preferred_element_type=F32))
            e_ref[i * blk:(i + 1) * blk, :] = e.astype(BF16)
            od_ref[i * blk:(i + 1) * blk, :] = o.astype(BF16)
            asum = asum + jnp.sum(e * sgn, axis=0, keepdims=True)
        zh = z_ref[0, h:h + 1, :].astype(F32) * inv
        zh_ref[...] = zh
        ab_ref[h:h + blk, :] = jnp.where(r == 0, asum * inv + zh, 0.0).astype(BF16)

    a = jnp.dot(ch_ref[...], e_ref[...], preferred_element_type=F32)
    b = jnp.dot(sh_ref[...], od_ref[...], preferred_element_type=F32)
    rr = lax.broadcasted_iota(jnp.int32, (tm, 1), 0)
    a = a + jnp.where((rr & 1) == 0, 1.0, -1.0) * zh_ref[...]
    rows = pl.ds(pl.multiple_of(mi * tm, tm), tm)
    o_ref[0, rows, :] = (a - b).astype(o_ref.dtype)
    ab_ref[rows, :] = (a + b).astype(BF16)

    @pl.when(mi == pl.num_programs(1) - 1)
    def _unfold():
        for p in range(h // blk):
            q = h // blk - p - 1
            win = ab_ref[q * blk:(q + 2) * blk, :]
            o_ref[0, h + p * blk:h + (p + 1) * blk, :] = jnp.dot(
                rev, win, preferred_element_type=F32).astype(o_ref.dtype)


def _position_dft(ch, sh, rev, z):
    b, s2, w = z.shape
    s = s2 // 2
    h = s // 2
    blk = rev.shape[0]
    assert h % blk == 0 and rev.shape[1] == 2 * blk
    tm = min(512, h)
    return pl.pallas_call(
        functools.partial(_dft_kernel, s, tm),
        grid=(b, h // tm),
        in_specs=[
            pl.BlockSpec((1, s2, w), lambda i, m: (i, 0, 0)),
            pl.BlockSpec((tm, h), lambda i, m: (m, 0)),
            pl.BlockSpec((tm, h), lambda i, m: (m, 0)),
            pl.BlockSpec((blk, 2 * blk), lambda i, m: (0, 0)),
        ],
        out_specs=pl.BlockSpec((1, s, w), lambda i, m: (i, 0, 0)),
        out_shape=jax.ShapeDtypeStruct((b, s, w), BF16),
        scratch_shapes=[pltpu.VMEM((h, w), BF16), pltpu.VMEM((h, w), BF16),
                        pltpu.VMEM((h + blk, w), BF16), pltpu.VMEM((1, w), F32)],
        compiler_params=pltpu.CompilerParams(
            dimension_semantics=("parallel", "arbitrary"),
            vmem_limit_bytes=V7X_VMEM_BYTES * 3 // 4),
        name="position_dft",
    )(z, ch, sh, rev)


def _scores_t(qpad, keys, masks):
    scores = []
    for kb, mk in zip(keys, masks):
        s = lax.dot_general(kb, qpad, (((1,), (1,)), ((), ())), preferred_element_type=F32)
        if mk is not None:
            s = jnp.where(mk, s, NEG)
        scores.append(s)
    return scores


def _softmax_pv_t(scores, vals_t, sink2):
    m = sink2
    for s in scores:
        m = jnp.maximum(m, jnp.max(s, axis=0, keepdims=True))
    acc = None
    for s, vt in zip(scores, vals_t):
        pv = jnp.dot(vt, jnp.exp2(s - m).astype(BF16), preferred_element_type=F32)
        acc = pv if acc is None else acc + pv
    den = acc[HEAD_DIM:HEAD_DIM + 1, :] + jnp.exp2(sink2 - m)
    return acc[:HEAD_DIM, :] / den


def _p2_kernel(latent, alpha, seq_len, nsub, *refs):
    if latent:
        (x_ref, h_ref, mod_ref, w2_ref, wb_ref, wo_ref, cw_ref, lng_ref, lnb_ref, sink_ref, yf_ref,
         um_ref, up_ref, un_ref, km_ref, kp_ref, kn_ref, vm_ref, vp_ref, vn_ref,
         kc_ref, vc_ref, cos_ref, sin_ref, y_ref) = refs
    else:
        (x_ref, h_ref, mod_ref, w2_ref, wb_ref, wo_ref, cw_ref, lng_ref, lnb_ref, sink_ref, yf_ref,
         um_ref, km_ref, vm_ref, y_ref) = refs
    t = pl.program_id(1)
    nt = pl.num_programs(1)
    tq = SUB_TILE
    tile = nsub * tq if latent else tq

    def window(main_ref, prev_ref, next_ref, lo, hi, axis):
        parts = []
        if lo < 0:
            parts.append(prev_ref[0])
            lo = 0
        inner = slice(lo, min(hi, tile))
        parts.append(main_ref[0, inner, :] if axis == 0 else main_ref[0, :, inner])
        if hi > tile:
            parts.append(next_ref[0])
        return parts[0] if len(parts) == 1 else jnp.concatenate(parts, axis=axis)

    def chain(sub):
        bi, r0 = (0, sub * tq) if latent else (sub, 0)
        rows = slice(r0, r0 + tq)
        hb = h_ref[bi, rows, :]

        def proj(off, width):
            return jnp.dot(hb, w2_ref[0, :, off:off + width], preferred_element_type=F32)

        q = proj(P2_Q, ATT_W)
        if latent:
            keys = [window(km_ref, kp_ref, kn_ref, r0 - WINDOW, r0 + tq + WINDOW, 0), kc_ref[0, 0]]
            vals = [window(vm_ref, vp_ref, vn_ref, r0 - WINDOW, r0 + tq + WINDOW, 1), vc_ref[0, 0]]
            nloc = keys[0].shape[0]
            j = lax.broadcasted_iota(jnp.int32, (nloc, tq), 0)
            r = lax.broadcasted_iota(jnp.int32, (nloc, tq), 1)
            dlt = j - r
            kpos = j + (t * tile + r0 - WINDOW)
            band = (dlt >= 0) & (dlt <= 2 * WINDOW) & (kpos >= 0) & (kpos < seq_len)
            masks = [band, None]
            cos = cos_ref[rows, :]
            sin = sin_ref[rows, :]
        else:
            keys = [km_ref[bi]]
            vals = [vm_ref[bi]]
            masks = [None]
        vals_lo, vals_hi = [], []
        for vt in vals:
            ones = jnp.ones((2 * SUBLANES, vt.shape[1]), BF16)
            vals_lo.append(jnp.concatenate([vt[:HEAD_DIM], ones], axis=0))
            vals_hi.append(jnp.concatenate([vt[HEAD_DIM:], ones], axis=0))
        lane = lax.broadcasted_iota(jnp.int32, (tq, LANES), 1)
        low = lane < HEAD_DIM
        heads = []
        for g in range(Q_PER_KV):
            qc = q[:, g * LANES:(g + 1) * LANES]
            if latent:
                qc = _rope(qc, cos, sin)
            qc = qc * (HEAD_DIM ** -0.5 * LOG2E)
            heads.append((jnp.where(low, qc, 0.0).astype(BF16), vals_lo, sink_ref[g:g + 1, 0:1] * LOG2E))
            heads.append((jnp.where(low, 0.0, qc).astype(BF16), vals_hi,
                          sink_ref[Q_PER_KV + g:Q_PER_KV + g + 1, 0:1] * LOG2E))
        side_cols = [(P2_ZA, ATT_W), (P2_CB, CONV_W), (P2_ZB, CONV_W), (P2_ZC, FOU_W)]
        side_cols += [(P2_G + br * D_MODEL, D_MODEL) for br in range(N_BRANCH)]
        side = []
        results = []
        sc = _scores_t(heads[0][0], keys, masks)
        for i, (_, vaug, sink2) in enumerate(heads):
            nxt = _scores_t(heads[i + 1][0], keys, masks) if i + 1 < len(heads) else None
            if i < len(side_cols):
                side.append(proj(*side_cols[i]))
            results.append(_softmax_pv_t(sc, vaug, sink2))
            sc = nxt
        p_za, p_cb, p_zb, p_zc = side[:4]
        p_gates = side[4:]
        ya = jnp.concatenate([jnp.concatenate(results[2 * g:2 * g + 2], axis=0).T for g in range(Q_PER_KV)],
                             axis=-1)
        y_a = ya * _silu(p_za)

        yield

        u = um_ref[bi, rows, :]
        if not latent:
            prev_row = next_row = jnp.zeros((1, CONV_W), F32)
        else:
            if r0 == 0:
                prev_row = jnp.where(t > 0, up_ref[0, SUBLANES - 1:SUBLANES, :], 0.0)
            else:
                prev_row = um_ref[0, r0 - 1:r0, :]
            if r0 + tq == tile:
                next_row = jnp.where(t < nt - 1, un_ref[0, 0:1, :], 0.0)
            else:
                next_row = um_ref[0, r0 + tq:r0 + tq + 1, :]
        row = lax.broadcasted_iota(jnp.int32, u.shape, 0)
        u_up = jnp.where(row == 0, prev_row, pltpu.roll(u, 1, 0))
        u_dn = jnp.where(row == tq - 1, next_row, pltpu.roll(u, tq - 1, 0))
        conv = u_up * cw_ref[0, 0:1, :] + u * cw_ref[0, 1:2, :] + u_dn * cw_ref[0, 2:3, :]
        y_b = p_cb * conv * _silu(p_zb)

        y_c = yf_ref[bi, rows, :].astype(F32) * _silu(p_zc)

        merged = None
        for br, yb in enumerate((y_a, y_b, y_c)):
            gt = jax.nn.sigmoid(p_gates[br])
            term = gt * jnp.dot(yb.astype(BF16), wb_ref[0, br], preferred_element_type=F32)
            merged = term if merged is None else merged + term
        out = jnp.dot(merged.astype(BF16), wo_ref[0], preferred_element_type=F32)

        yield

        res = alpha * x_ref[bi, rows, :] + mod_ref[0, 0, 2:3, :] * out
        y_ref[bi, rows, :] = _layernorm(res) * lng_ref[0] + lnb_ref[0]
        yield

    chains = [chain(sub) for sub in range(nsub)]
    for _stage in range(3):
        for ch in chains:
            next(ch)


def _pass2(x, hb, mod, mod_row, layer, w2, wb, wo, conv_w, ln_g, ln_b, sink_b, yf, u, k, v,
           cache_k, cache_v, rope_tabs, latent, alpha):
    b, s, d = x.shape
    nsub = P2_SUB_TILES
    if latent:
        bb, t = 1, nsub * SUB_TILE
        assert s % t == 0 and SUB_TILE == 2 * WINDOW
    else:
        bb, t = nsub, s
        assert s == SUB_TILE and b % nsub == 0
    nt = s // t
    if mod_row is None:
        mod_idx = lambda i, j: (layer, i, 0, 0)
    else:
        mod_idx = lambda i, j: (layer, mod_row, 0, 0)
    const3 = lambda i, j: (layer, 0, 0)
    tile3 = lambda i, j: (i, j, 0)
    in_specs = [
        pl.BlockSpec((bb, t, d), tile3),
        pl.BlockSpec((bb, t, d), tile3),
        pl.BlockSpec((1, 1, 3, d), mod_idx),
        pl.BlockSpec((1, d, W2_COLS), const3),
        pl.BlockSpec((1, N_BRANCH, BRANCH_W, d), lambda i, j: (layer, 0, 0, 0)),
        pl.BlockSpec((1, d, d), const3),
        pl.BlockSpec((1, 3, CONV_W), const3),
        pl.BlockSpec((1, 1, d), const3),
        pl.BlockSpec((1, 1, d), const3),
        pl.BlockSpec((N_HEADS, LANES), lambda i, j: (layer, 0)),
        pl.BlockSpec((bb, t, FOU_W), tile3),
        pl.BlockSpec((bb, t, CONV_W), tile3),
    ]
    args = [x, hb, mod, w2, wb, wo, conv_w, ln_g, ln_b, sink_b, yf, u]
    if latent:
        rpt = t // SUBLANES
        wpt = t // WINDOW
        nb8 = s // SUBLANES
        nbw = s // WINDOW
        prev8 = lambda i, j: (i, jnp.maximum(j * rpt - 1, 0), 0)
        next8 = lambda i, j: (i, jnp.minimum((j + 1) * rpt, nb8 - 1), 0)
        prevw = lambda i, j: (i, jnp.maximum(j * wpt - 1, 0), 0)
        nextw = lambda i, j: (i, jnp.minimum((j + 1) * wpt, nbw - 1), 0)
        in_specs += [pl.BlockSpec((1, SUBLANES, CONV_W), prev8), pl.BlockSpec((1, SUBLANES, CONV_W), next8)]
        args += [u, u]
        in_specs += [pl.BlockSpec((1, t, KV_W), tile3),
                     pl.BlockSpec((1, WINDOW, KV_W), prevw),
                     pl.BlockSpec((1, WINDOW, KV_W), nextw)]
        args += [k, k, k]
        swap = lambda f: (lambda i, j: (f(i, j)[0], 0, f(i, j)[1]))
        in_specs += [pl.BlockSpec((1, KV_W, t), swap(tile3)),
                     pl.BlockSpec((1, KV_W, WINDOW), swap(prevw)),
                     pl.BlockSpec((1, KV_W, WINDOW), swap(nextw))]
        args += [v, v, v]
        past = cache_k.shape[2]
        in_specs += [pl.BlockSpec((1, 1, past, KV_W), lambda i, j: (i, layer, 0, 0)),
                     pl.BlockSpec((1, 1, KV_W, past), lambda i, j: (i, layer, 0, 0))]
        args += [cache_k, cache_v]
        in_specs += [pl.BlockSpec((t, KV_W), lambda i, j: (j, 0))] * 2
        args += list(rope_tabs)
    else:
        in_specs += [pl.BlockSpec((bb, t, KV_W), tile3), pl.BlockSpec((bb, KV_W, t), lambda i, j: (i, 0, j))]
        args += [k, v]
    return pl.pallas_call(
        functools.partial(_p2_kernel, latent, alpha, s, nsub),
        grid=(b // bb, nt),
        in_specs=in_specs,
        out_specs=pl.BlockSpec((bb, t, d), tile3),
        out_shape=jax.ShapeDtypeStruct((b, s, d), F32),
        compiler_params=pltpu.CompilerParams(
            dimension_semantics=("parallel", "parallel"),
            vmem_limit_bytes=V7X_VMEM_BYTES * 7 // 8),
        name="pass2_latent" if latent else "pass2_context",
    )(*args)


def _dft_tables(n, m):
    idx = jnp.arange(m, dtype=jnp.int32)
    r = 64
    if m <= 256 or m % r:
        ang = (2.0 * np.pi / n) * ((idx[:, None] * idx[None, :]) % n).astype(F32)
        return jnp.cos(ang), jnp.sin(ang)
    hi = jnp.arange(m // r, dtype=jnp.int32)
    lo = jnp.arange(r, dtype=jnp.int32)
    a = (2.0 * np.pi / n) * ((r * hi[:, None] * idx[None, :]) % n).astype(F32)
    bb = (2.0 * np.pi / n) * ((lo[:, None] * idx[None, :]) % n).astype(F32)
    ca, sa = jnp.cos(a)[:, None, :], jnp.sin(a)[:, None, :]
    cb, sb = jnp.cos(bb)[None, :, :], jnp.sin(bb)[None, :, :]
    return (ca * cb - sa * sb).reshape(m, m), (sa * cb + ca * sb).reshape(m, m)


def _position_tables(s):
    c, sn = _dft_tables(s, s // 2)
    scale = s ** -0.5
    return (c * scale).astype(BF16), (sn * scale).astype(BF16)


def _reversal_matrix():
    r = np.arange(REV_BLOCK)[:, None]
    c = np.arange(2 * REV_BLOCK)[None, :]
    return jnp.asarray(c == REV_BLOCK - r, BF16)


def _channel_matrix():
    idx = np.arange(FOU_GW)
    ang = (2.0 * np.pi / FOU_GW) * ((idx[:, None] * idx[None, :]) % FOU_GW)
    tab = np.concatenate([np.cos(ang), np.sin(ang)], axis=1) * (FOU_GW ** -0.5)
    return jnp.asarray(tab, F32).astype(BF16)


def _rope_tables(n_tokens):
    pos = np.arange(n_tokens)
    row = (pos // GRID_W).astype(np.float64)
    col = (pos % GRID_W).astype(np.float64)
    n_freq = ROT_AXIS // 2
    inv_freq = ROPE_BASE ** (-np.arange(n_freq, dtype=np.float64) / n_freq)
    lane = np.arange(LANES)
    dd = lane % HEAD_DIM
    ang = np.where((dd < ROT_AXIS)[None, :], row[:, None], col[:, None]) * inv_freq[dd % n_freq][None, :]
    sign = np.where((lane % ROT_AXIS) < n_freq, -1.0, 1.0)
    return jnp.asarray(np.cos(ang), F32), jnp.asarray(np.sin(ang) * sign[None, :], F32)


def _permute_heads(w, axis):
    shp = w.shape
    w = w.reshape(shp[:axis] + (KV_HEADS, Q_PER_KV, HEAD_DIM) + shp[axis + 1:])
    return jnp.swapaxes(w, axis, axis + 1).reshape(shp)


def kernel(x_prompt, x_sample, cache_k, cache_v, c, c_ctx, w_mod, b_mod, w_in, sink, conv_w, w_branch, w_o, ln_g, ln_b):
    depth = w_in.shape[0]
    alpha = float((2 * depth) ** 0.25)
    dec_b, dec_s, d = x_sample.shape
    ctx_b, ctx_s, _ = x_prompt.shape

    rows = -(-(dec_b + 1) // SUBLANES) * SUBLANES
    cond = jnp.concatenate([c, c_ctx[None, :], jnp.zeros((rows - dec_b - 1, d), F32)], axis=0)
    mod = _modulation(cond, w_mod, b_mod).reshape(depth, rows, 3, d)

    sl = lambda a, bnd: w_in[:, :, a:bnd]
    w1 = jnp.concatenate([sl(O_K, O_V), sl(O_V, O_ZA), sl(O_CC, O_CX), sl(O_CX, O_ZB), sl(O_FX, O_ZC)],
                         axis=-1).astype(BF16)
    w2 = jnp.concatenate([_permute_heads(sl(O_Q, O_K), 2), _permute_heads(sl(O_ZA, O_CB), 2), sl(O_CB, O_CC),
                          sl(O_ZB, O_FX), sl(O_ZC, O_G), sl(O_G, IN_W)], axis=-1).astype(BF16)
    wb = jnp.concatenate([_permute_heads(w_branch[:, 0:1], 2), w_branch[:, 1:]], axis=1).astype(BF16)
    wo = w_o.astype(BF16)
    sink_b = jnp.broadcast_to(sink.reshape(depth * N_HEADS, 1), (depth * N_HEADS, LANES))
    ln_g3 = ln_g.reshape(depth, 1, d)
    ln_b3 = ln_b.reshape(depth, 1, d)

    dftc = _channel_matrix()
    rev = _reversal_matrix()
    rope_tabs = _rope_tables(dec_s)
    past = cache_k.shape[2]
    ck, cv = _prepare_cache(cache_k.reshape(dec_b, depth, past, KV_W), cache_v.reshape(dec_b, depth, past, KV_W))

    def run_group(x, mod_row, latent):
        b, s, _ = x.shape
        ch, sh = _position_tables(s)
        y = x
        ks, vs = [], []
        for l in range(depth):
            hb, kb, vt, u, z, *kv32 = _pass1(y, mod, mod_row, l, w1, dftc, rope_tabs, latent)
            yf = _position_dft(ch, sh, rev, z.reshape(b, 2 * s, FOU_W))
            y = _pass2(y, hb, mod, mod_row, l, w2, wb, wo, conv_w, ln_g3, ln_b3, sink_b, yf, u, kb, vt,
                       ck, cv, rope_tabs, latent, alpha)
            if kv32:
                ks.append(kv32[0])
                vs.append(kv32[1])
        return y, ks, vs

    y_prompt, ks, vs = run_group(x_prompt, dec_b, False)
    new_k = jnp.stack(ks, axis=1).reshape(ctx_b, depth, ctx_s, KV_HEADS, HEAD_DIM)
    new_v = jnp.stack(vs, axis=1).reshape(ctx_b, depth, ctx_s, KV_HEADS, HEAD_DIM)
    y_sample, _, _ = run_group(x_sample, None, True)
    return (y_prompt, y_sample, new_k, new_v)
```

```python
import functools

import numpy as np
import jax
import jax.numpy as jnp
from jax import lax
from jax.experimental import pallas as pl
from jax.experimental.pallas import tpu as pltpu

F32 = jnp.float32
BF16 = jnp.bfloat16

D_MODEL = 1024
GRID_W = 64
N_HEADS = 8
KV_HEADS = 2
HEAD_DIM = 64
Q_PER_KV = N_HEADS // KV_HEADS
ATT_W = N_HEADS * HEAD_DIM
KV_W = KV_HEADS * HEAD_DIM
WINDOW = 128
CONV_W = 512
FOU_GROUPS = 4
FOU_GW = 128
FOU_W = FOU_GROUPS * FOU_GW
BRANCH_W = 512
N_BRANCH = 3
ROT_AXIS = HEAD_DIM // 2
ROPE_BASE = 10000.0
LN_EPS = 1e-6
NEG = -1e30
LOG2E = 1.4426950408889634

_OFF = np.cumsum([0, ATT_W, KV_W, KV_W, ATT_W, CONV_W, CONV_W, CONV_W, CONV_W, FOU_W, FOU_W, N_BRANCH * D_MODEL])
(O_Q, O_K, O_V, O_ZA, O_CB, O_CC, O_CX, O_ZB, O_FX, O_ZC, O_G, IN_W) = [int(v) for v in _OFF]

W1_COLS = 2 * KV_W + 2 * CONV_W + FOU_W
W2_COLS = 2 * ATT_W + 2 * CONV_W + FOU_W + N_BRANCH * D_MODEL
P2_Q, P2_ZA, P2_CB, P2_ZB, P2_ZC, P2_G = 0, 512, 1024, 1536, 2048, 2560

V7X_VMEM_BYTES = 64 * 1024 * 1024
LANES = 128
SUBLANES = 8

P1_TOKEN_TILE = 1024
SUB_TILE = 256
P2_SUB_TILES = 2
REV_BLOCK = 128
MOD_COL_TILE = 768


def _layernorm(x):
    mu = jnp.mean(x, axis=-1, keepdims=True)
    xc = x - mu
    var = jnp.mean(xc * xc, axis=-1, keepdims=True)
    return xc * lax.rsqrt(var + LN_EPS)


def _silu(z):
    return z * jax.nn.sigmoid(z)


def _rope(x, cos, sin_signed):
    lane = lax.broadcasted_iota(jnp.int32, x.shape, 1)
    partner = jnp.where((lane & 16) == 0, pltpu.roll(x, LANES - 16, 1), pltpu.roll(x, 16, 1))
    return x * cos + partner * sin_signed


def _mod_kernel(c_ref, w_ref, b_ref, o_ref):
    c = c_ref[...]
    a = _silu(c).astype(BF16)
    o_ref[0] = jnp.dot(a, w_ref[0].astype(BF16), preferred_element_type=F32) + b_ref[0]


def _modulation(cond, w_mod, b_mod):
    depth, d, n = w_mod.shape
    rows = cond.shape[0]
    return pl.pallas_call(
        _mod_kernel,
        grid=(depth, n // MOD_COL_TILE),
        in_specs=[
            pl.BlockSpec((rows, d), lambda l, j: (0, 0)),
            pl.BlockSpec((1, d, MOD_COL_TILE), lambda l, j: (l, 0, j)),
            pl.BlockSpec((1, 1, MOD_COL_TILE), lambda l, j: (l, 0, j)),
        ],
        out_specs=pl.BlockSpec((1, rows, MOD_COL_TILE), lambda l, j: (l, 0, j)),
        out_shape=jax.ShapeDtypeStruct((depth, rows, n), F32),
        compiler_params=pltpu.CompilerParams(dimension_semantics=("parallel", "parallel")),
        name="modulation",
    )(cond, w_mod, b_mod.reshape(depth, 1, n))


def _p1_kernel(latent, *refs):
    if latent:
        x_ref, mod_ref, w1_ref, dftc_ref, cos_ref, sin_ref, h_ref, kb_ref, vt_ref, u_ref, z_ref = refs
    else:
        x_ref, mod_ref, w1_ref, dftc_ref, h_ref, kb_ref, vt_ref, u_ref, z_ref, k_ref, v_ref = refs
    shift = mod_ref[0, 0, 0:1, :]
    scale = mod_ref[0, 0, 1:2, :]
    bb, t = x_ref.shape[:2]
    if bb > 1:
        units = [(i, slice(0, t)) for i in range(bb)]
    else:
        units = [(0, slice(0, t // 2)), (0, slice(t // 2, t))]

    def chain(bi, rows):
        hb = (_layernorm(x_ref[bi, rows, :]) * (1.0 + scale) + shift).astype(BF16)
        h_ref[bi, rows, :] = hb
        yield
        p = jnp.dot(hb, w1_ref[0], preferred_element_type=F32)
        k = p[:, 0:KV_W]
        v = p[:, KV_W:2 * KV_W]
        if latent:
            k = _rope(k, cos_ref[rows, :], sin_ref[rows, :])
        else:
            k_ref[bi, rows, :] = k
            v_ref[bi, rows, :] = v
        kb_ref[bi, rows, :] = k.astype(BF16)
        vt_ref[bi, :, rows] = v.T.astype(BF16)
        o = 2 * KV_W
        u_ref[bi, rows, :] = p[:, o:o + CONV_W] * p[:, o + CONV_W:o + 2 * CONV_W]
        o += 2 * CONV_W
        for g in range(FOU_GROUPS):
            fg = p[:, o + g * FOU_GW:o + (g + 1) * FOU_GW].astype(BF16)
            zg = jnp.dot(fg, dftc_ref[...], preferred_element_type=F32)
            z_ref[bi, 0, rows, g * FOU_GW:(g + 1) * FOU_GW] = zg[:, :FOU_GW].astype(BF16)
            z_ref[bi, 1, rows, g * FOU_GW:(g + 1) * FOU_GW] = zg[:, FOU_GW:].astype(BF16)
        yield

    chains = [chain(bi, rows) for bi, rows in units]
    for _stage in range(2):
        for ch in chains:
            next(ch)


def _pass1(x, mod, mod_row, layer, w1, dftc, rope_tabs, latent):
    b, s, d = x.shape
    t = min(P1_TOKEN_TILE, s)
    bb = P1_TOKEN_TILE // t if mod_row is not None else 1
    assert s % t == 0 and b % bb == 0 and (bb == 1 or t == s)
    nt = s // t
    if mod_row is None:
        mod_idx = lambda i, j: (layer, i, 0, 0)
    else:
        mod_idx = lambda i, j: (layer, mod_row, 0, 0)
    in_specs = [
        pl.BlockSpec((bb, t, d), lambda i, j: (i, j, 0)),
        pl.BlockSpec((1, 1, 3, d), mod_idx),
        pl.BlockSpec((1, d, W1_COLS), lambda i, j: (layer, 0, 0)),
        pl.BlockSpec((FOU_GW, 2 * FOU_GW), lambda i, j: (0, 0)),
    ]
    args = [x, mod, w1, dftc]
    if latent:
        in_specs += [pl.BlockSpec((t, KV_W), lambda i, j: (j, 0))] * 2
        args += list(rope_tabs)
    out_shape = [
        jax.ShapeDtypeStruct((b, s, d), BF16),
        jax.ShapeDtypeStruct((b, s, KV_W), BF16),
        jax.ShapeDtypeStruct((b, KV_W, s), BF16),
        jax.ShapeDtypeStruct((b, s, CONV_W), F32),
        jax.ShapeDtypeStruct((b, 2, s, FOU_W), BF16),
    ]
    out_specs = [
        pl.BlockSpec((bb, t, d), lambda i, j: (i, j, 0)),
        pl.BlockSpec((bb, t, KV_W), lambda i, j: (i, j, 0)),
        pl.BlockSpec((bb, KV_W, t), lambda i, j: (i, 0, j)),
        pl.BlockSpec((bb, t, CONV_W), lambda i, j: (i, j, 0)),
        pl.BlockSpec((bb, 2, t, FOU_W), lambda i, j: (i, 0, j, 0)),
    ]
    if not latent:
        out_shape += [jax.ShapeDtypeStruct((b, s, KV_W), F32)] * 2
        out_specs += [pl.BlockSpec((bb, t, KV_W), lambda i, j: (i, j, 0))] * 2
    return pl.pallas_call(
        functools.partial(_p1_kernel, latent),
        grid=(b // bb, nt),
        in_specs=in_specs,
        out_specs=out_specs,
        out_shape=out_shape,
        compiler_params=pltpu.CompilerParams(
            dimension_semantics=("parallel", "parallel"),
            vmem_limit_bytes=V7X_VMEM_BYTES * 3 // 4),
        name="pass1_latent" if latent else "pass1_context",
    )(*args)


def _cache_kernel(k_ref, v_ref, kb_ref, vt_ref):
    kb_ref[0, 0] = k_ref[0, 0].astype(BF16)
    vt_ref[0, 0] = v_ref[0, 0].T.astype(BF16)


def _prepare_cache(cache_k, cache_v):
    b, depth, past, w = cache_k.shape
    spec = pl.BlockSpec((1, 1, past, w), lambda i, l: (i, l, 0, 0))
    return pl.pallas_call(
        _cache_kernel,
        grid=(b, depth),
        in_specs=[spec, spec],
        out_specs=[spec, pl.BlockSpec((1, 1, w, past), lambda i, l: (i, l, 0, 0))],
        out_shape=[jax.ShapeDtypeStruct((b, depth, past, w), BF16),
                   jax.ShapeDtypeStruct((b, depth, w, past), BF16)],
        compiler_params=pltpu.CompilerParams(dimension_semantics=("parallel", "parallel")),
        name="cache_prep",
    )(cache_k, cache_v)


def _dft_kernel(n, tm, z_ref, ch_ref, sh_ref, rev_ref, o_ref, e_ref, od_ref, ab_ref, zh_ref):
    h = n // 2
    blk = rev_ref.shape[0]
    nblk = n // blk
    mi = pl.program_id(1)
    inv = n ** -0.5
    rev = rev_ref[...]

    @pl.when(mi == 0)
    def _fold():
        r = lax.broadcasted_iota(jnp.int32, (blk, 1), 0)
        sgn = jnp.where((r & 1) == 0, 1.0, -1.0)
        asum = jnp.zeros((1, e_ref.shape[1]), F32)
        for i in range(h // blk):
            q = nblk - 1 - i
            if i == 0:
                pad = jnp.zeros((blk, e_ref.shape[1]), BF16)
                mir_r = jnp.concatenate([z_ref[0, q * blk:(q + 1) * blk, :], pad], axis=0)
                mir_i = jnp.concatenate([z_ref[0, n + q * blk:n + (q + 1) * blk, :], pad], axis=0)
            else:
                mir_r = z_ref[0, q * blk:(q + 2) * blk, :]
                mir_i = z_ref[0, n + q * blk:n + (q + 2) * blk, :]
            e = z_ref[0, i * blk:(i + 1) * blk, :].astype(F32) + jnp.dot(rev, mir_r, preferred_element_type=F32)
            o = (z_ref[0, n + i * blk:n + (i + 1) * blk, :].astype(F32)
                 - jnp.dot(rev, mir_i, preferred_element_type=F32))
            e_ref[i * blk:(i + 1) * blk, :] = e.astype(BF16)
            od_ref[i * blk:(i + 1) * blk, :] = o.astype(BF16)
            asum = asum + jnp.sum(e * sgn, axis=0, keepdims=True)
        zh = z_ref[0, h:h + 1, :].astype(F32) * inv
        zh_ref[...] = zh
        ab_ref[h:h + blk, :] = jnp.where(r == 0, asum * inv + zh, 0.0).astype(BF16)

    a = jnp.dot(ch_ref[...], e_ref[...], preferred_element_type=F32)
    b = jnp.dot(sh_ref[...], od_ref[...], preferred_element_type=F32)
    rr = lax.broadcasted_iota(jnp.int32, (tm, 1), 0)
    a = a + jnp.where((rr & 1) == 0, 1.0, -1.0) * zh_ref[...]
    rows = pl.ds(pl.multiple_of(mi * tm, tm), tm)
    o_ref[0, rows, :] = (a - b).astype(o_ref.dtype)
    ab_ref[rows, :] = (a + b).astype(BF16)

    @pl.when(mi == pl.num_programs(1) - 1)
    def _unfold():
        for p in range(h // blk):
            q = h // blk - p - 1
            win = ab_ref[q * blk:(q + 2) * blk, :]
            o_ref[0, h + p * blk:h + (p + 1) * blk, :] = jnp.dot(
                rev, win, preferred_element_type=F32).astype(o_ref.dtype)


def _position_dft(ch, sh, rev, z):
    b, s2, w = z.shape
    s = s2 // 2
    h = s // 2
    blk = rev.shape[0]
    assert h % blk == 0 and rev.shape[1] == 2 * blk
    tm = min(512, h)
    return pl.pallas_call(
        functools.partial(_dft_kernel, s, tm),
        grid=(b, h // tm),
        in_specs=[
            pl.BlockSpec((1, s2, w), lambda i, m: (i, 0, 0)),
            pl.BlockSpec((tm, h), lambda i, m: (m, 0)),
            pl.BlockSpec((tm, h), lambda i, m: (m, 0)),
            pl.BlockSpec((blk, 2 * blk), lambda i, m: (0, 0)),
        ],
        out_specs=pl.BlockSpec((1, s, w), lambda i, m: (i, 0, 0)),
        out_shape=jax.ShapeDtypeStruct((b, s, w), BF16),
        scratch_shapes=[pltpu.VMEM((h, w), BF16), pltpu.VMEM((h, w), BF16),
                        pltpu.VMEM((h + blk, w), BF16), pltpu.VMEM((1, w), F32)],
        compiler_params=pltpu.CompilerParams(
            dimension_semantics=("parallel", "arbitrary"),
            vmem_limit_bytes=V7X_VMEM_BYTES * 3 // 4),
        name="position_dft",
    )(z, ch, sh, rev)


def _scores_t(qpad, keys, masks):
    scores = []
    for kb, mk in zip(keys, masks):
        s = lax.dot_general(kb, qpad, (((1,), (1,)), ((), ())), preferred_element_type=F32)
        if mk is not None:
            s = jnp.where(mk, s, NEG)
        scores.append(s)
    return scores


def _softmax_pv_t(scores, vals_t, sink2):
    m = sink2
    for s in scores:
        m = jnp.maximum(m, jnp.max(s, axis=0, keepdims=True))
    acc = None
    for s, vt in zip(scores, vals_t):
        pv = jnp.dot(vt, jnp.exp2(s - m).astype(BF16), preferred_element_type=F32)
        acc = pv if acc is None else acc + pv
    den = acc[HEAD_DIM:HEAD_DIM + 1, :] + jnp.exp2(sink2 - m)
    return acc[:HEAD_DIM, :] / den


def _p2_kernel(latent, alpha, seq_len, nsub, *refs):
    if latent:
        (x_ref, h_ref, mod_ref, w2_ref, wb_ref, wo_ref, cw_ref, lng_ref, lnb_ref, sink_ref, yf_ref,
         um_ref, up_ref, un_ref, km_ref, kp_ref, kn_ref, vm_ref, vp_ref, vn_ref,
         kc_ref, vc_ref, cos_ref, sin_ref, y_ref) = refs
    else:
        (x_ref, h_ref, mod_ref, w2_ref, wb_ref, wo_ref, cw_ref, lng_ref, lnb_ref, sink_ref, yf_ref,
         um_ref, km_ref, vm_ref, y_ref) = refs
    t = pl.program_id(1)
    nt = pl.num_programs(1)
    tq = SUB_TILE
    tile = nsub * tq if latent else tq

    def window(main_ref, prev_ref, next_ref, lo, hi, axis):
        parts = []
        if lo < 0:
            parts.append(prev_ref[0])
            lo = 0
        inner = slice(lo, min(hi, tile))
        parts.append(main_ref[0, inner, :] if axis == 0 else main_ref[0, :, inner])
        if hi > tile:
            parts.append(next_ref[0])
        return parts[0] if len(parts) == 1 else jnp.concatenate(parts, axis=axis)

    def chain(sub):
        bi, r0 = (0, sub * tq) if latent else (sub, 0)
        rows = slice(r0, r0 + tq)
        hb = h_ref[bi, rows, :]

        def proj(off, width):
            return jnp.dot(hb, w2_ref[0, :, off:off + width], preferred_element_type=F32)

        q = proj(P2_Q, ATT_W)
        if latent:
            keys = [window(km_ref, kp_ref, kn_ref, r0 - WINDOW, r0 + tq + WINDOW, 0), kc_ref[0, 0]]
            vals = [window(vm_ref, vp_ref, vn_ref, r0 - WINDOW, r0 + tq + WINDOW, 1), vc_ref[0, 0]]
            nloc = keys[0].shape[0]
            j = lax.broadcasted_iota(jnp.int32, (nloc, tq), 0)
            r = lax.broadcasted_iota(jnp.int32, (nloc, tq), 1)
            dlt = j - r
            kpos = j + (t * tile + r0 - WINDOW)
            band = (dlt >= 0) & (dlt <= 2 * WINDOW) & (kpos >= 0) & (kpos < seq_len)
            masks = [band, None]
            cos = cos_ref[rows, :]
            sin = sin_ref[rows, :]
        else:
            keys = [km_ref[bi]]
            vals = [vm_ref[bi]]
            masks = [None]
        vals_lo, vals_hi = [], []
        for vt in vals:
            ones = jnp.ones((2 * SUBLANES, vt.shape[1]), BF16)
            vals_lo.append(jnp.concatenate([vt[:HEAD_DIM], ones], axis=0))
            vals_hi.append(jnp.concatenate([vt[HEAD_DIM:], ones], axis=0))
        lane = lax.broadcasted_iota(jnp.int32, (tq, LANES), 1)
        low = lane < HEAD_DIM
        heads = []
        for g in range(Q_PER_KV):
            qc = q[:, g * LANES:(g + 1) * LANES]
            if latent:
                qc = _rope(qc, cos, sin)
            qc = qc * (HEAD_DIM ** -0.5 * LOG2E)
            heads.append((jnp.where(low, qc, 0.0).astype(BF16), vals_lo, sink_ref[g:g + 1, 0:1] * LOG2E))
            heads.append((jnp.where(low, 0.0, qc).astype(BF16), vals_hi,
                          sink_ref[Q_PER_KV + g:Q_PER_KV + g + 1, 0:1] * LOG2E))
        yield

        side_cols = [(P2_ZA, ATT_W), (P2_CB, CONV_W), (P2_ZB, CONV_W), (P2_ZC, FOU_W)]
        side_cols += [(P2_G + br * D_MODEL, D_MODEL) for br in range(N_BRANCH)]
        side = []
        results = []
        sc = _scores_t(heads[0][0], keys, masks)
        for i, (_, vaug, sink2) in enumerate(heads):
            nxt = _scores_t(heads[i + 1][0], keys, masks) if i + 1 < len(heads) else None
            if i < len(side_cols):
                side.append(proj(*side_cols[i]))
            results.append(_softmax_pv_t(sc, vaug, sink2))
            sc = nxt
        p_za, p_cb, p_zb, p_zc = side[:4]
        p_gates = side[4:]
        ya = jnp.concatenate([jnp.concatenate(results[2 * g:2 * g + 2], axis=0).T for g in range(Q_PER_KV)],
                             axis=-1)
        y_a = ya * _silu(p_za)

        yield

        u = um_ref[bi, rows, :]
        if not latent:
            prev_row = next_row = jnp.zeros((1, CONV_W), F32)
        else:
            if r0 == 0:
                prev_row = jnp.where(t > 0, up_ref[0, SUBLANES - 1:SUBLANES, :], 0.0)
            else:
                prev_row = um_ref[0, r0 - 1:r0, :]
            if r0 + tq == tile:
                next_row = jnp.where(t < nt - 1, un_ref[0, 0:1, :], 0.0)
            else:
                next_row = um_ref[0, r0 + tq:r0 + tq + 1, :]
        row = lax.broadcasted_iota(jnp.int32, u.shape, 0)
        u_up = jnp.where(row == 0, prev_row, pltpu.roll(u, 1, 0))
        u_dn = jnp.where(row == tq - 1, next_row, pltpu.roll(u, tq - 1, 0))
        conv = u_up * cw_ref[0, 0:1, :] + u * cw_ref[0, 1:2, :] + u_dn * cw_ref[0, 2:3, :]
        y_b = p_cb * conv * _silu(p_zb)

        y_c = yf_ref[bi, rows, :].astype(F32) * _silu(p_zc)

        merged = None
        for br, yb in enumerate((y_a, y_b, y_c)):
            gt = jax.nn.sigmoid(p_gates[br])
            term = gt * jnp.dot(yb.astype(BF16), wb_ref[0, br], preferred_element_type=F32)
            merged = term if merged is None else merged + term
        out = jnp.dot(merged.astype(BF16), wo_ref[0], preferred_element_type=F32)

        yield

        res = alpha * x_ref[bi, rows, :] + mod_ref[0, 0, 2:3, :] * out
        y_ref[bi, rows, :] = _layernorm(res) * lng_ref[0] + lnb_ref[0]
        yield

    chains = [chain(sub) for sub in range(nsub)]
    for _stage in range(4):
        for ch in chains:
            next(ch)


def _pass2(x, hb, mod, mod_row, layer, w2, wb, wo, conv_w, ln_g, ln_b, sink_b, yf, u, k, v,
           cache_k, cache_v, rope_tabs, latent, alpha):
    b, s, d = x.shape
    nsub = P2_SUB_TILES
    if latent:
        bb, t = 1, nsub * SUB_TILE
        assert s % t == 0 and SUB_TILE == 2 * WINDOW
    else:
        bb, t = nsub, s
        assert s == SUB_TILE and b % nsub == 0
    nt = s // t
    if mod_row is None:
        mod_idx = lambda i, j: (layer, i, 0, 0)
    else:
        mod_idx = lambda i, j: (layer, mod_row, 0, 0)
    const3 = lambda i, j: (layer, 0, 0)
    tile3 = lambda i, j: (i, j, 0)
    in_specs = [
        pl.BlockSpec((bb, t, d), tile3),
        pl.BlockSpec((bb, t, d), tile3),
        pl.BlockSpec((1, 1, 3, d), mod_idx),
        pl.BlockSpec((1, d, W2_COLS), const3),
        pl.BlockSpec((1, N_BRANCH, BRANCH_W, d), lambda i, j: (layer, 0, 0, 0)),
        pl.BlockSpec((1, d, d), const3),
        pl.BlockSpec((1, 3, CONV_W), const3),
        pl.BlockSpec((1, 1, d), const3),
        pl.BlockSpec((1, 1, d), const3),
        pl.BlockSpec((N_HEADS, LANES), lambda i, j: (layer, 0)),
        pl.BlockSpec((bb, t, FOU_W), tile3),
        pl.BlockSpec((bb, t, CONV_W), tile3),
    ]
    args = [x, hb, mod, w2, wb, wo, conv_w, ln_g, ln_b, sink_b, yf, u]
    if latent:
        rpt = t // SUBLANES
        wpt = t // WINDOW
        nb8 = s // SUBLANES
        nbw = s // WINDOW
        prev8 = lambda i, j: (i, jnp.maximum(j * rpt - 1, 0), 0)
        next8 = lambda i, j: (i, jnp.minimum((j + 1) * rpt, nb8 - 1), 0)
        prevw = lambda i, j: (i, jnp.maximum(j * wpt - 1, 0), 0)
        nextw = lambda i, j: (i, jnp.minimum((j + 1) * wpt, nbw - 1), 0)
        in_specs += [pl.BlockSpec((1, SUBLANES, CONV_W), prev8), pl.BlockSpec((1, SUBLANES, CONV_W), next8)]
        args += [u, u]
        in_specs += [pl.BlockSpec((1, t, KV_W), tile3),
                     pl.BlockSpec((1, WINDOW, KV_W), prevw),
                     pl.BlockSpec((1, WINDOW, KV_W), nextw)]
        args += [k, k, k]
        swap = lambda f: (lambda i, j: (f(i, j)[0], 0, f(i, j)[1]))
        in_specs += [pl.BlockSpec((1, KV_W, t), swap(tile3)),
                     pl.BlockSpec((1, KV_W, WINDOW), swap(prevw)),
                     pl.BlockSpec((1, KV_W, WINDOW), swap(nextw))]
        args += [v, v, v]
        past = cache_k.shape[2]
        in_specs += [pl.BlockSpec((1, 1, past, KV_W), lambda i, j: (i, layer, 0, 0)),
                     pl.BlockSpec((1, 1, KV_W, past), lambda i, j: (i, layer, 0, 0))]
        args += [cache_k, cache_v]
        in_specs += [pl.BlockSpec((t, KV_W), lambda i, j: (j, 0))] * 2
        args += list(rope_tabs)
    else:
        in_specs += [pl.BlockSpec((bb, t, KV_W), tile3), pl.BlockSpec((bb, KV_W, t), lambda i, j: (i, 0, j))]
        args += [k, v]
    return pl.pallas_call(
        functools.partial(_p2_kernel, latent, alpha, s, nsub),
        grid=(b // bb, nt),
        in_specs=in_specs,
        out_specs=pl.BlockSpec((bb, t, d), tile3),
        out_shape=jax.ShapeDtypeStruct((b, s, d), F32),
        compiler_params=pltpu.CompilerParams(
            dimension_semantics=("parallel", "parallel"),
            vmem_limit_bytes=V7X_VMEM_BYTES * 7 // 8),
        name="pass2_latent" if latent else "pass2_context",
    )(*args)


def _dft_tables(n, m):
    idx = jnp.arange(m, dtype=jnp.int32)
    r = 64
    if m <= 256 or m % r:
        ang = (2.0 * np.pi / n) * ((idx[:, None] * idx[None, :]) % n).astype(F32)
        return jnp.cos(ang), jnp.sin(ang)
    hi = jnp.arange(m // r, dtype=jnp.int32)
    lo = jnp.arange(r, dtype=jnp.int32)
    a = (2.0 * np.pi / n) * ((r * hi[:, None] * idx[None, :]) % n).astype(F32)
    bb = (2.0 * np.pi / n) * ((lo[:, None] * idx[None, :]) % n).astype(F32)
    ca, sa = jnp.cos(a)[:, None, :], jnp.sin(a)[:, None, :]
    cb, sb = jnp.cos(bb)[None, :, :], jnp.sin(bb)[None, :, :]
    return (ca * cb - sa * sb).reshape(m, m), (sa * cb + ca * sb).reshape(m, m)


def _position_tables(s):
    c, sn = _dft_tables(s, s // 2)
    scale = s ** -0.5
    return (c * scale).astype(BF16), (sn * scale).astype(BF16)


def _reversal_matrix():
    r = np.arange(REV_BLOCK)[:, None]
    c = np.arange(2 * REV_BLOCK)[None, :]
    return jnp.asarray(c == REV_BLOCK - r, BF16)


def _channel_matrix():
    idx = np.arange(FOU_GW)
    ang = (2.0 * np.pi / FOU_GW) * ((idx[:, None] * idx[None, :]) % FOU_GW)
    tab = np.concatenate([np.cos(ang), np.sin(ang)], axis=1) * (FOU_GW ** -0.5)
    return jnp.asarray(tab, F32).astype(BF16)


def _rope_tables(n_tokens):
    pos = np.arange(n_tokens)
    row = (pos // GRID_W).astype(np.float64)
    col = (pos % GRID_W).astype(np.float64)
    n_freq = ROT_AXIS // 2
    inv_freq = ROPE_BASE ** (-np.arange(n_freq, dtype=np.float64) / n_freq)
    lane = np.arange(LANES)
    dd = lane % HEAD_DIM
    ang = np.where((dd < ROT_AXIS)[None, :], row[:, None], col[:, None]) * inv_freq[dd % n_freq][None, :]
    sign = np.where((lane % ROT_AXIS) < n_freq, -1.0, 1.0)
    return jnp.asarray(np.cos(ang), F32), jnp.asarray(np.sin(ang) * sign[None, :], F32)


def _permute_heads(w, axis):
    shp = w.shape
    w = w.reshape(shp[:axis] + (KV_HEADS, Q_PER_KV, HEAD_DIM) + shp[axis + 1:])
    return jnp.swapaxes(w, axis, axis + 1).reshape(shp)


def kernel(x_prompt, x_sample, cache_k, cache_v, c, c_ctx, w_mod, b_mod, w_in, sink, conv_w, w_branch, w_o, ln_g, ln_b):
    depth = w_in.shape[0]
    alpha = float((2 * depth) ** 0.25)
    dec_b, dec_s, d = x_sample.shape
    ctx_b, ctx_s, _ = x_prompt.shape

    rows = -(-(dec_b + 1) // SUBLANES) * SUBLANES
    cond = jnp.concatenate([c, c_ctx[None, :], jnp.zeros((rows - dec_b - 1, d), F32)], axis=0)
    mod = _modulation(cond, w_mod, b_mod).reshape(depth, rows, 3, d)

    sl = lambda a, bnd: w_in[:, :, a:bnd]
    w1 = jnp.concatenate([sl(O_K, O_V), sl(O_V, O_ZA), sl(O_CC, O_CX), sl(O_CX, O_ZB), sl(O_FX, O_ZC)],
                         axis=-1).astype(BF16)
    w2 = jnp.concatenate([_permute_heads(sl(O_Q, O_K), 2), _permute_heads(sl(O_ZA, O_CB), 2), sl(O_CB, O_CC),
                          sl(O_ZB, O_FX), sl(O_ZC, O_G), sl(O_G, IN_W)], axis=-1).astype(BF16)
    wb = jnp.concatenate([_permute_heads(w_branch[:, 0:1], 2), w_branch[:, 1:]], axis=1).astype(BF16)
    wo = w_o.astype(BF16)
    sink_b = jnp.broadcast_to(sink.reshape(depth * N_HEADS, 1), (depth * N_HEADS, LANES))
    ln_g3 = ln_g.reshape(depth, 1, d)
    ln_b3 = ln_b.reshape(depth, 1, d)

    dftc = _channel_matrix()
    rev = _reversal_matrix()
    rope_tabs = _rope_tables(dec_s)
    past = cache_k.shape[2]
    ck, cv = _prepare_cache(cache_k.reshape(dec_b, depth, past, KV_W), cache_v.reshape(dec_b, depth, past, KV_W))

    def run_group(x, mod_row, latent):
        b, s, _ = x.shape
        ch, sh = _position_tables(s)
        y = x
        ks, vs = [], []
        for l in range(depth):
            hb, kb, vt, u, z, *kv32 = _pass1(y, mod, mod_row, l, w1, dftc, rope_tabs, latent)
            yf = _position_dft(ch, sh, rev, z.reshape(b, 2 * s, FOU_W))
            y = _pass2(y, hb, mod, mod_row, l, w2, wb, wo, conv_w, ln_g3, ln_b3, sink_b, yf, u, kb, vt,
                       ck, cv, rope_tabs, latent, alpha)
            if kv32:
                ks.append(kv32[0])
                vs.append(kv32[1])
        return y, ks, vs

    y_prompt, ks, vs = run_group(x_prompt, dec_b, False)
    new_k = jnp.stack(ks, axis=1).reshape(ctx_b, depth, ctx_s, KV_HEADS, HEAD_DIM)
    new_v = jnp.stack(vs, axis=1).reshape(ctx_b, depth, ctx_s, KV_HEADS, HEAD_DIM)
    y_sample, _, _ = run_group(x_sample, None, True)
    return (y_prompt, y_sample, new_k, new_v)
```

```python
import functools

import numpy as np
import jax
import jax.numpy as jnp
from jax import lax
from jax.experimental import pallas as pl
from jax.experimental.pallas import tpu as pltpu

F32 = jnp.float32
BF16 = jnp.bfloat16

D_MODEL = 1024
GRID_W = 64
N_HEADS = 8
KV_HEADS = 2
HEAD_DIM = 64
Q_PER_KV = N_HEADS // KV_HEADS
ATT_W = N_HEADS * HEAD_DIM
KV_W = KV_HEADS * HEAD_DIM
WINDOW = 128
CONV_W = 512
FOU_GROUPS = 4
FOU_GW = 128
FOU_W = FOU_GROUPS * FOU_GW
BRANCH_W = 512
N_BRANCH = 3
ROT_AXIS = HEAD_DIM // 2
ROPE_BASE = 10000.0
LN_EPS = 1e-6
NEG = -1e30
LOG2E = 1.4426950408889634

_OFF = np.cumsum([0, ATT_W, KV_W, KV_W, ATT_W, CONV_W, CONV_W, CONV_W, CONV_W, FOU_W, FOU_W, N_BRANCH * D_MODEL])
(O_Q, O_K, O_V, O_ZA, O_CB, O_CC, O_CX, O_ZB, O_FX, O_ZC, O_G, IN_W) = [int(v) for v in _OFF]


V7X_VMEM_BYTES = 64 * 1024 * 1024
LANES = 128
SUBLANES = 8

P1_TOKEN_TILE = 1024
SUB_TILE = 256
P2_SUB_TILES = 2
REV_BLOCK = 128
MOD_COL_TILE = 768


def _layernorm(x):
    mu = jnp.mean(x, axis=-1, keepdims=True)
    xc = x - mu
    var = jnp.mean(xc * xc, axis=-1, keepdims=True)
    return xc * lax.rsqrt(var + LN_EPS)


def _silu(z):
    return z * jax.nn.sigmoid(z)


def _rope(x, cos, sin_signed):
    lane = lax.broadcasted_iota(jnp.int32, x.shape, 1)
    partner = jnp.where((lane & 16) == 0, pltpu.roll(x, LANES - 16, 1), pltpu.roll(x, 16, 1))
    return x * cos + partner * sin_signed


def _mod_kernel(c_ref, w_ref, b_ref, o_ref):
    c = c_ref[...]
    a = _silu(c).astype(BF16)
    o_ref[0] = jnp.dot(a, w_ref[0].astype(BF16), preferred_element_type=F32) + b_ref[0]


def _modulation(cond, w_mod, b_mod):
    depth, d, n = w_mod.shape
    rows = cond.shape[0]
    return pl.pallas_call(
        _mod_kernel,
        grid=(depth, n // MOD_COL_TILE),
        in_specs=[
            pl.BlockSpec((rows, d), lambda l, j: (0, 0)),
            pl.BlockSpec((1, d, MOD_COL_TILE), lambda l, j: (l, 0, j)),
            pl.BlockSpec((1, 1, MOD_COL_TILE), lambda l, j: (l, 0, j)),
        ],
        out_specs=pl.BlockSpec((1, rows, MOD_COL_TILE), lambda l, j: (l, 0, j)),
        out_shape=jax.ShapeDtypeStruct((depth, rows, n), F32),
        compiler_params=pltpu.CompilerParams(dimension_semantics=("parallel", "parallel")),
        name="modulation",
    )(cond, w_mod, b_mod.reshape(depth, 1, n))


def _p1_kernel(latent, *refs):
    if latent:
        x_ref, mod_ref, w1_ref, dftc_ref, cos_ref, sin_ref, h_ref, kb_ref, vt_ref, u_ref, z_ref = refs
    else:
        x_ref, mod_ref, w1_ref, dftc_ref, h_ref, kb_ref, vt_ref, u_ref, z_ref, k_ref, v_ref = refs
    shift = mod_ref[0, 0, 0:1, :]
    scale = mod_ref[0, 0, 1:2, :]
    bb, t = x_ref.shape[:2]
    if bb > 1:
        units = [(i, slice(0, t)) for i in range(bb)]
    else:
        units = [(0, slice(0, t // 2)), (0, slice(t // 2, t))]

    def chain(bi, rows):
        hb = (_layernorm(x_ref[bi, rows, :]) * (1.0 + scale) + shift).astype(BF16)
        h_ref[bi, rows, :] = hb
        yield
        def proj(lo, hi):
            return jnp.dot(hb, w1_ref[0, :, lo:hi], preferred_element_type=F32)

        kv = proj(O_K, O_ZA)
        k = kv[:, 0:KV_W]
        v = kv[:, KV_W:2 * KV_W]
        if latent:
            k = _rope(k, cos_ref[rows, :], sin_ref[rows, :])
        else:
            k_ref[bi, rows, :] = k
            v_ref[bi, rows, :] = v
        kb_ref[bi, rows, :] = k.astype(BF16)
        vt_ref[bi, :, rows] = v.T.astype(BF16)
        cu = proj(O_CC, O_ZB)
        u_ref[bi, rows, :] = cu[:, :CONV_W] * cu[:, CONV_W:]
        fx = proj(O_FX, O_ZC)
        for g in range(FOU_GROUPS):
            fg = fx[:, g * FOU_GW:(g + 1) * FOU_GW].astype(BF16)
            zg = jnp.dot(fg, dftc_ref[...], preferred_element_type=F32)
            z_ref[bi, 0, rows, g * FOU_GW:(g + 1) * FOU_GW] = zg[:, :FOU_GW].astype(BF16)
            z_ref[bi, 1, rows, g * FOU_GW:(g + 1) * FOU_GW] = zg[:, FOU_GW:].astype(BF16)
        yield

    chains = [chain(bi, rows) for bi, rows in units]
    for _stage in range(2):
        for ch in chains:
            next(ch)


def _pass1(x, mod, mod_row, layer, w1, dftc, rope_tabs, latent):
    b, s, d = x.shape
    t = min(P1_TOKEN_TILE, s)
    bb = P1_TOKEN_TILE // t if mod_row is not None else 1
    assert s % t == 0 and b % bb == 0 and (bb == 1 or t == s)
    nt = s // t
    if mod_row is None:
        mod_idx = lambda i, j: (layer, i, 0, 0)
    else:
        mod_idx = lambda i, j: (layer, mod_row, 0, 0)
    in_specs = [
        pl.BlockSpec((bb, t, d), lambda i, j: (i, j, 0)),
        pl.BlockSpec((1, 1, 3, d), mod_idx),
        pl.BlockSpec((1, d, IN_W), lambda i, j: (layer, 0, 0)),
        pl.BlockSpec((FOU_GW, 2 * FOU_GW), lambda i, j: (0, 0)),
    ]
    args = [x, mod, w1, dftc]
    if latent:
        in_specs += [pl.BlockSpec((t, KV_W), lambda i, j: (j, 0))] * 2
        args += list(rope_tabs)
    out_shape = [
        jax.ShapeDtypeStruct((b, s, d), BF16),
        jax.ShapeDtypeStruct((b, s, KV_W), BF16),
        jax.ShapeDtypeStruct((b, KV_W, s), BF16),
        jax.ShapeDtypeStruct((b, s, CONV_W), F32),
        jax.ShapeDtypeStruct((b, 2, s, FOU_W), BF16),
    ]
    out_specs = [
        pl.BlockSpec((bb, t, d), lambda i, j: (i, j, 0)),
        pl.BlockSpec((bb, t, KV_W), lambda i, j: (i, j, 0)),
        pl.BlockSpec((bb, KV_W, t), lambda i, j: (i, 0, j)),
        pl.BlockSpec((bb, t, CONV_W), lambda i, j: (i, j, 0)),
        pl.BlockSpec((bb, 2, t, FOU_W), lambda i, j: (i, 0, j, 0)),
    ]
    if not latent:
        out_shape += [jax.ShapeDtypeStruct((b, s, KV_W), F32)] * 2
        out_specs += [pl.BlockSpec((bb, t, KV_W), lambda i, j: (i, j, 0))] * 2
    return pl.pallas_call(
        functools.partial(_p1_kernel, latent),
        grid=(b // bb, nt),
        in_specs=in_specs,
        out_specs=out_specs,
        out_shape=out_shape,
        compiler_params=pltpu.CompilerParams(
            dimension_semantics=("parallel", "parallel"),
            vmem_limit_bytes=V7X_VMEM_BYTES * 3 // 4),
        name="pass1_latent" if latent else "pass1_context",
    )(*args)


def _cache_kernel(k_ref, v_ref, kb_ref, vt_ref):
    kb_ref[0, 0] = k_ref[0, 0].astype(BF16)
    vt_ref[0, 0] = v_ref[0, 0].T.astype(BF16)


def _prepare_cache(cache_k, cache_v):
    b, depth, past, w = cache_k.shape
    spec = pl.BlockSpec((1, 1, past, w), lambda i, l: (i, l, 0, 0))
    return pl.pallas_call(
        _cache_kernel,
        grid=(b, depth),
        in_specs=[spec, spec],
        out_specs=[spec, pl.BlockSpec((1, 1, w, past), lambda i, l: (i, l, 0, 0))],
        out_shape=[jax.ShapeDtypeStruct((b, depth, past, w), BF16),
                   jax.ShapeDtypeStruct((b, depth, w, past), BF16)],
        compiler_params=pltpu.CompilerParams(dimension_semantics=("parallel", "parallel")),
        name="cache_prep",
    )(cache_k, cache_v)


def _dft_kernel(n, tm, z_ref, ch_ref, sh_ref, rev_ref, o_ref, e_ref, od_ref, ab_ref, zh_ref):
    h = n // 2
    blk = rev_ref.shape[0]
    nblk = n // blk
    mi = pl.program_id(1)
    inv = n ** -0.5
    rev = rev_ref[...]

    @pl.when(mi == 0)
    def _fold():
        r = lax.broadcasted_iota(jnp.int32, (blk, 1), 0)
        sgn = jnp.where((r & 1) == 0, 1.0, -1.0)
        asum = jnp.zeros((1, e_ref.shape[1]), F32)
        for i in range(h // blk):
            q = nblk - 1 - i
            if i == 0:
                pad = jnp.zeros((blk, e_ref.shape[1]), BF16)
                mir_r = jnp.concatenate([z_ref[0, q * blk:(q + 1) * blk, :], pad], axis=0)
                mir_i = jnp.concatenate([z_ref[0, n + q * blk:n + (q + 1) * blk, :], pad], axis=0)
            else:
                mir_r = z_ref[0, q * blk:(q + 2) * blk, :]
                mir_i = z_ref[0, n + q * blk:n + (q + 2) * blk, :]
            e = z_ref[0, i * blk:(i + 1) * blk, :].astype(F32) + jnp.dot(rev, mir_r, preferred_element_type=F32)
            o = (z_ref[0, n + i * blk:n + (i + 1) * blk, :].astype(F32)
                 - jnp.dot(rev, mir_i, preferred_element_type=F32))
            e_ref[i * blk:(i + 1) * blk, :] = e.astype(BF16)
            od_ref[i * blk:(i + 1) * blk, :] = o.astype(BF16)
            asum = asum + jnp.sum(e * sgn, axis=0, keepdims=True)
        zh = z_ref[0, h:h + 1, :].astype(F32) * inv
        zh_ref[...] = zh
        ab_ref[h:h + blk, :] = jnp.where(r == 0, asum * inv + zh, 0.0).astype(BF16)

    a = jnp.dot(ch_ref[...], e_ref[...], preferred_element_type=F32)
    b = jnp.dot(sh_ref[...], od_ref[...], preferred_element_type=F32)
    rr = lax.broadcasted_iota(jnp.int32, (tm, 1), 0)
    a = a + jnp.where((rr & 1) == 0, 1.0, -1.0) * zh_ref[...]
    rows = pl.ds(pl.multiple_of(mi * tm, tm), tm)
    o_ref[0, rows, :] = (a - b).astype(o_ref.dtype)
    ab_ref[rows, :] = (a + b).astype(BF16)

    @pl.when(mi == pl.num_programs(1) - 1)
    def _unfold():
        for p in range(h // blk):
            q = h // blk - p - 1
            win = ab_ref[q * blk:(q + 2) * blk, :]
            o_ref[0, h + p * blk:h + (p + 1) * blk, :] = jnp.dot(
                rev, win, preferred_element_type=F32).astype(o_ref.dtype)


def _position_dft(ch, sh, rev, z):
    b, s2, w = z.shape
    s = s2 // 2
    h = s // 2
    blk = rev.shape[0]
    assert h % blk == 0 and rev.shape[1] == 2 * blk
    tm = min(512, h)
    return pl.pallas_call(
        functools.partial(_dft_kernel, s, tm),
        grid=(b, h // tm),
        in_specs=[
            pl.BlockSpec((1, s2, w), lambda i, m: (i, 0, 0)),
            pl.BlockSpec((tm, h), lambda i, m: (m, 0)),
            pl.BlockSpec((tm, h), lambda i, m: (m, 0)),
            pl.BlockSpec((blk, 2 * blk), lambda i, m: (0, 0)),
        ],
        out_specs=pl.BlockSpec((1, s, w), lambda i, m: (i, 0, 0)),
        out_shape=jax.ShapeDtypeStruct((b, s, w), BF16),
        scratch_shapes=[pltpu.VMEM((h, w), BF16), pltpu.VMEM((h, w), BF16),
                        pltpu.VMEM((h + blk, w), BF16), pltpu.VMEM((1, w), F32)],
        compiler_params=pltpu.CompilerParams(
            dimension_semantics=("parallel", "arbitrary"),
            vmem_limit_bytes=V7X_VMEM_BYTES * 3 // 4),
        name="position_dft",
    )(z, ch, sh, rev)


def _scores_t(qpad, keys, masks):
    scores = []
    for kb, mk in zip(keys, masks):
        s = lax.dot_general(kb, qpad, (((1,), (1,)), ((), ())), preferred_element_type=F32)
        if mk is not None:
            s = jnp.where(mk, s, NEG)
        scores.append(s)
    return scores


def _softmax_pv_t(scores, vals_t, sink2):
    m = sink2
    for s in scores:
        m = jnp.maximum(m, jnp.max(s, axis=0, keepdims=True))
    acc = None
    for s, vt in zip(scores, vals_t):
        pv = jnp.dot(vt, jnp.exp2(s - m).astype(BF16), preferred_element_type=F32)
        acc = pv if acc is None else acc + pv
    den = acc[HEAD_DIM:HEAD_DIM + 1, :] + jnp.exp2(sink2 - m)
    return acc[:HEAD_DIM, :] / den


def _p2_kernel(latent, alpha, seq_len, nsub, *refs):
    if latent:
        (x_ref, h_ref, mod_ref, w2_ref, wb_ref, wo_ref, cw_ref, lng_ref, lnb_ref, sink_ref, yf_ref,
         um_ref, up_ref, un_ref, km_ref, kp_ref, kn_ref, vm_ref, vp_ref, vn_ref,
         kc_ref, vc_ref, cos_ref, sin_ref, y_ref) = refs
    else:
        (x_ref, h_ref, mod_ref, w2_ref, wb_ref, wo_ref, cw_ref, lng_ref, lnb_ref, sink_ref, yf_ref,
         um_ref, km_ref, vm_ref, y_ref) = refs
    t = pl.program_id(1)
    nt = pl.num_programs(1)
    tq = SUB_TILE
    tile = nsub * tq if latent else tq

    def window(main_ref, prev_ref, next_ref, lo, hi, axis):
        parts = []
        if lo < 0:
            parts.append(prev_ref[0])
            lo = 0
        inner = slice(lo, min(hi, tile))
        parts.append(main_ref[0, inner, :] if axis == 0 else main_ref[0, :, inner])
        if hi > tile:
            parts.append(next_ref[0])
        return parts[0] if len(parts) == 1 else jnp.concatenate(parts, axis=axis)

    def chain(sub):
        bi, r0 = (0, sub * tq) if latent else (sub, 0)
        rows = slice(r0, r0 + tq)
        hb = h_ref[bi, rows, :]

        def proj(off, width):
            return jnp.dot(hb, w2_ref[0, :, off:off + width], preferred_element_type=F32)

        q = proj(O_Q, ATT_W)
        if latent:
            keys = [window(km_ref, kp_ref, kn_ref, r0 - WINDOW, r0 + tq + WINDOW, 0), kc_ref[0, 0]]
            vals = [window(vm_ref, vp_ref, vn_ref, r0 - WINDOW, r0 + tq + WINDOW, 1), vc_ref[0, 0]]
            nloc = keys[0].shape[0]
            j = lax.broadcasted_iota(jnp.int32, (nloc, tq), 0)
            r = lax.broadcasted_iota(jnp.int32, (nloc, tq), 1)
            dlt = j - r
            kpos = j + (t * tile + r0 - WINDOW)
            band = (dlt >= 0) & (dlt <= 2 * WINDOW) & (kpos >= 0) & (kpos < seq_len)
            masks = [band, None]
            cos = cos_ref[rows, :]
            sin = sin_ref[rows, :]
        else:
            keys = [km_ref[bi]]
            vals = [vm_ref[bi]]
            masks = [None]
        vals_lo, vals_hi = [], []
        for vt in vals:
            ones = jnp.ones((2 * SUBLANES, vt.shape[1]), BF16)
            vals_lo.append(jnp.concatenate([vt[:HEAD_DIM], ones], axis=0))
            vals_hi.append(jnp.concatenate([vt[HEAD_DIM:], ones], axis=0))
        lane = lax.broadcasted_iota(jnp.int32, (tq, LANES), 1)
        low = lane < HEAD_DIM
        qblocks = []
        for b in range(ATT_W // LANES):
            qb = q[:, b * LANES:(b + 1) * LANES]
            if latent:
                qb = _rope(qb, cos, sin)
            qblocks.append(qb * (HEAD_DIM ** -0.5 * LOG2E))
        heads = []
        for g in range(Q_PER_KV):
            for kv in range(KV_HEADS):
                qb = qblocks[kv * (Q_PER_KV // 2) + g // 2]
                if g % 2 != kv:
                    qb = pltpu.roll(qb, HEAD_DIM, 1)
                qpad = jnp.where(low, qb, 0.0) if kv == 0 else jnp.where(low, 0.0, qb)
                h = kv * Q_PER_KV + g
                heads.append((qpad.astype(BF16), vals_lo if kv == 0 else vals_hi, sink_ref[h:h + 1, 0:1] * LOG2E))
        yield

        side_cols = [(O_ZA, ATT_W), (O_CB, CONV_W), (O_ZB, CONV_W), (O_ZC, FOU_W)]
        side_cols += [(O_G + br * D_MODEL, D_MODEL) for br in range(N_BRANCH)]
        side = []
        results = []
        sc = _scores_t(heads[0][0], keys, masks)
        for i, (_, vaug, sink2) in enumerate(heads):
            nxt = _scores_t(heads[i + 1][0], keys, masks) if i + 1 < len(heads) else None
            if i < len(side_cols):
                side.append(proj(*side_cols[i]))
            results.append(_softmax_pv_t(sc, vaug, sink2))
            sc = nxt
        p_za, p_cb, p_zb, p_zc = side[:4]
        p_gates = side[4:]
        out_t = {(i // KV_HEADS, i % KV_HEADS): o for i, o in enumerate(results)}
        ya = jnp.concatenate(
            [jnp.concatenate([out_t[(g, kv)], out_t[(g + 1, kv)]], axis=0).T
             for kv in range(KV_HEADS) for g in range(0, Q_PER_KV, 2)], axis=-1)
        y_a = ya * _silu(p_za)

        yield

        u = um_ref[bi, rows, :]
        if not latent:
            prev_row = next_row = jnp.zeros((1, CONV_W), F32)
        else:
            if r0 == 0:
                prev_row = jnp.where(t > 0, up_ref[0, SUBLANES - 1:SUBLANES, :], 0.0)
            else:
                prev_row = um_ref[0, r0 - 1:r0, :]
            if r0 + tq == tile:
                next_row = jnp.where(t < nt - 1, un_ref[0, 0:1, :], 0.0)
            else:
                next_row = um_ref[0, r0 + tq:r0 + tq + 1, :]
        row = lax.broadcasted_iota(jnp.int32, u.shape, 0)
        u_up = jnp.where(row == 0, prev_row, pltpu.roll(u, 1, 0))
        u_dn = jnp.where(row == tq - 1, next_row, pltpu.roll(u, tq - 1, 0))
        conv = u_up * cw_ref[0, 0:1, :] + u * cw_ref[0, 1:2, :] + u_dn * cw_ref[0, 2:3, :]
        y_b = p_cb * conv * _silu(p_zb)

        y_c = yf_ref[bi, rows, :].astype(F32) * _silu(p_zc)

        merged = None
        for br, yb in enumerate((y_a, y_b, y_c)):
            gt = jax.nn.sigmoid(p_gates[br])
            term = gt * jnp.dot(yb.astype(BF16), wb_ref[0, br], preferred_element_type=F32)
            merged = term if merged is None else merged + term
        out = jnp.dot(merged.astype(BF16), wo_ref[0], preferred_element_type=F32)

        yield

        res = alpha * x_ref[bi, rows, :] + mod_ref[0, 0, 2:3, :] * out
        y_ref[bi, rows, :] = _layernorm(res) * lng_ref[0] + lnb_ref[0]
        yield

    chains = [chain(sub) for sub in range(nsub)]
    for _stage in range(4):
        for ch in chains:
            next(ch)


def _pass2(x, hb, mod, mod_row, layer, w2, wb, wo, conv_w, ln_g, ln_b, sink_b, yf, u, k, v,
           cache_k, cache_v, rope_tabs, latent, alpha):
    b, s, d = x.shape
    nsub = P2_SUB_TILES
    if latent:
        bb, t = 1, nsub * SUB_TILE
        assert s % t == 0 and SUB_TILE == 2 * WINDOW
    else:
        bb, t = nsub, s
        assert s == SUB_TILE and b % nsub == 0
    nt = s // t
    if mod_row is None:
        mod_idx = lambda i, j: (layer, i, 0, 0)
    else:
        mod_idx = lambda i, j: (layer, mod_row, 0, 0)
    const3 = lambda i, j: (layer, 0, 0)
    tile3 = lambda i, j: (i, j, 0)
    in_specs = [
        pl.BlockSpec((bb, t, d), tile3),
        pl.BlockSpec((bb, t, d), tile3),
        pl.BlockSpec((1, 1, 3, d), mod_idx),
        pl.BlockSpec((1, d, IN_W), const3),
        pl.BlockSpec((1, N_BRANCH, BRANCH_W, d), lambda i, j: (layer, 0, 0, 0)),
        pl.BlockSpec((1, d, d), const3),
        pl.BlockSpec((1, 3, CONV_W), const3),
        pl.BlockSpec((1, 1, d), const3),
        pl.BlockSpec((1, 1, d), const3),
        pl.BlockSpec((N_HEADS, LANES), lambda i, j: (layer, 0)),
        pl.BlockSpec((bb, t, FOU_W), tile3),
        pl.BlockSpec((bb, t, CONV_W), tile3),
    ]
    args = [x, hb, mod, w2, wb, wo, conv_w, ln_g, ln_b, sink_b, yf, u]
    if latent:
        rpt = t // SUBLANES
        wpt = t // WINDOW
        nb8 = s // SUBLANES
        nbw = s // WINDOW
        prev8 = lambda i, j: (i, jnp.maximum(j * rpt - 1, 0), 0)
        next8 = lambda i, j: (i, jnp.minimum((j + 1) * rpt, nb8 - 1), 0)
        prevw = lambda i, j: (i, jnp.maximum(j * wpt - 1, 0), 0)
        nextw = lambda i, j: (i, jnp.minimum((j + 1) * wpt, nbw - 1), 0)
        in_specs += [pl.BlockSpec((1, SUBLANES, CONV_W), prev8), pl.BlockSpec((1, SUBLANES, CONV_W), next8)]
        args += [u, u]
        in_specs += [pl.BlockSpec((1, t, KV_W), tile3),
                     pl.BlockSpec((1, WINDOW, KV_W), prevw),
                     pl.BlockSpec((1, WINDOW, KV_W), nextw)]
        args += [k, k, k]
        swap = lambda f: (lambda i, j: (f(i, j)[0], 0, f(i, j)[1]))
        in_specs += [pl.BlockSpec((1, KV_W, t), swap(tile3)),
                     pl.BlockSpec((1, KV_W, WINDOW), swap(prevw)),
                     pl.BlockSpec((1, KV_W, WINDOW), swap(nextw))]
        args += [v, v, v]
        past = cache_k.shape[2]
        in_specs += [pl.BlockSpec((1, 1, past, KV_W), lambda i, j: (i, layer, 0, 0)),
                     pl.BlockSpec((1, 1, KV_W, past), lambda i, j: (i, layer, 0, 0))]
        args += [cache_k, cache_v]
        in_specs += [pl.BlockSpec((t, KV_W), lambda i, j: (j, 0))] * 2
        args += list(rope_tabs)
    else:
        in_specs += [pl.BlockSpec((bb, t, KV_W), tile3), pl.BlockSpec((bb, KV_W, t), lambda i, j: (i, 0, j))]
        args += [k, v]
    return pl.pallas_call(
        functools.partial(_p2_kernel, latent, alpha, s, nsub),
        grid=(b // bb, nt),
        in_specs=in_specs,
        out_specs=pl.BlockSpec((bb, t, d), tile3),
        out_shape=jax.ShapeDtypeStruct((b, s, d), F32),
        compiler_params=pltpu.CompilerParams(
            dimension_semantics=("parallel", "parallel"),
            vmem_limit_bytes=V7X_VMEM_BYTES * 7 // 8),
        name="pass2_latent" if latent else "pass2_context",
    )(*args)


def _dft_tables(n, m):
    idx = jnp.arange(m, dtype=jnp.int32)
    r = 64
    if m <= 256 or m % r:
        ang = (2.0 * np.pi / n) * ((idx[:, None] * idx[None, :]) % n).astype(F32)
        return jnp.cos(ang), jnp.sin(ang)
    hi = jnp.arange(m // r, dtype=jnp.int32)
    lo = jnp.arange(r, dtype=jnp.int32)
    a = (2.0 * np.pi / n) * ((r * hi[:, None] * idx[None, :]) % n).astype(F32)
    bb = (2.0 * np.pi / n) * ((lo[:, None] * idx[None, :]) % n).astype(F32)
    ca, sa = jnp.cos(a)[:, None, :], jnp.sin(a)[:, None, :]
    cb, sb = jnp.cos(bb)[None, :, :], jnp.sin(bb)[None, :, :]
    return (ca * cb - sa * sb).reshape(m, m), (sa * cb + ca * sb).reshape(m, m)


def _position_tables(s):
    c, sn = _dft_tables(s, s // 2)
    scale = s ** -0.5
    return (c * scale).astype(BF16), (sn * scale).astype(BF16)


def _reversal_matrix():
    r = np.arange(REV_BLOCK)[:, None]
    c = np.arange(2 * REV_BLOCK)[None, :]
    return jnp.asarray(c == REV_BLOCK - r, BF16)


def _channel_matrix():
    idx = np.arange(FOU_GW)
    ang = (2.0 * np.pi / FOU_GW) * ((idx[:, None] * idx[None, :]) % FOU_GW)
    tab = np.concatenate([np.cos(ang), np.sin(ang)], axis=1) * (FOU_GW ** -0.5)
    return jnp.asarray(tab, F32).astype(BF16)


def _rope_tables(n_tokens):
    pos = np.arange(n_tokens)
    row = (pos // GRID_W).astype(np.float64)
    col = (pos % GRID_W).astype(np.float64)
    n_freq = ROT_AXIS // 2
    inv_freq = ROPE_BASE ** (-np.arange(n_freq, dtype=np.float64) / n_freq)
    lane = np.arange(LANES)
    dd = lane % HEAD_DIM
    ang = np.where((dd < ROT_AXIS)[None, :], row[:, None], col[:, None]) * inv_freq[dd % n_freq][None, :]
    sign = np.where((lane % ROT_AXIS) < n_freq, -1.0, 1.0)
    return jnp.asarray(np.cos(ang), F32), jnp.asarray(np.sin(ang) * sign[None, :], F32)


def kernel(x_prompt, x_sample, cache_k, cache_v, c, c_ctx, w_mod, b_mod, w_in, sink, conv_w, w_branch, w_o, ln_g, ln_b):
    depth = w_in.shape[0]
    alpha = float((2 * depth) ** 0.25)
    dec_b, dec_s, d = x_sample.shape
    ctx_b, ctx_s, _ = x_prompt.shape

    rows = -(-(dec_b + 1) // SUBLANES) * SUBLANES
    cond = jnp.concatenate([c, c_ctx[None, :], jnp.zeros((rows - dec_b - 1, d), F32)], axis=0)
    mod = _modulation(cond, w_mod, b_mod).reshape(depth, rows, 3, d)

    w_all = w_in.astype(BF16)
    wb = w_branch.astype(BF16)
    wo = w_o.astype(BF16)
    sink_b = jnp.broadcast_to(sink.reshape(depth * N_HEADS, 1), (depth * N_HEADS, LANES))
    ln_g3 = ln_g.reshape(depth, 1, d)
    ln_b3 = ln_b.reshape(depth, 1, d)

    dftc = _channel_matrix()
    rev = _reversal_matrix()
    rope_tabs = _rope_tables(dec_s)
    past = cache_k.shape[2]
    ck, cv = _prepare_cache(cache_k.reshape(dec_b, depth, past, KV_W), cache_v.reshape(dec_b, depth, past, KV_W))

    def run_group(x, mod_row, latent):
        b, s, _ = x.shape
        ch, sh = _position_tables(s)
        y = x
        ks, vs = [], []
        for l in range(depth):
            hb, kb, vt, u, z, *kv32 = _pass1(y, mod, mod_row, l, w_all, dftc, rope_tabs, latent)
            yf = _position_dft(ch, sh, rev, z.reshape(b, 2 * s, FOU_W))
            y = _pass2(y, hb, mod, mod_row, l, w_all, wb, wo, conv_w, ln_g3, ln_b3, sink_b, yf, u, kb, vt,
                       ck, cv, rope_tabs, latent, alpha)
            if kv32:
                ks.append(kv32[0])
                vs.append(kv32[1])
        return y, ks, vs

    y_prompt, ks, vs = run_group(x_prompt, dec_b, False)
    new_k = jnp.stack(ks, axis=1).reshape(ctx_b, depth, ctx_s, KV_HEADS, HEAD_DIM)
    new_v = jnp.stack(vs, axis=1).reshape(ctx_b, depth, ctx_s, KV_HEADS, HEAD_DIM)
    y_sample, _, _ = run_group(x_sample, None, True)
    return (y_prompt, y_sample, new_k, new_v)
```

```python
import functools

import numpy as np
import jax
import jax.numpy as jnp
from jax import lax
from jax.experimental import pallas as pl
from jax.experimental.pallas import tpu as pltpu

F32 = jnp.float32
BF16 = jnp.bfloat16

D_MODEL = 1024
GRID_W = 64
N_HEADS = 8
KV_HEADS = 2
HEAD_DIM = 64
Q_PER_KV = N_HEADS // KV_HEADS
ATT_W = N_HEADS * HEAD_DIM
KV_W = KV_HEADS * HEAD_DIM
WINDOW = 128
CONV_W = 512
FOU_GROUPS = 4
FOU_GW = 128
FOU_W = FOU_GROUPS * FOU_GW
BRANCH_W = 512
N_BRANCH = 3
ROT_AXIS = HEAD_DIM // 2
ROPE_BASE = 10000.0
LN_EPS = 1e-6
NEG = -1e30
LOG2E = 1.4426950408889634

_OFF = np.cumsum([0, ATT_W, KV_W, KV_W, ATT_W, CONV_W, CONV_W, CONV_W, CONV_W, FOU_W, FOU_W, N_BRANCH * D_MODEL])
(O_Q, O_K, O_V, O_ZA, O_CB, O_CC, O_CX, O_ZB, O_FX, O_ZC, O_G, IN_W) = [int(v) for v in _OFF]


V7X_VMEM_BYTES = 64 * 1024 * 1024
LANES = 128
SUBLANES = 8

P1_TOKEN_TILE = 1024
SUB_TILE = 256
P2_SUB_TILES = 2
REV_BLOCK = 128
DFT_ROW_TILE = 512
DFT_TABLE_FACTOR = 64
MOD_COL_TILE = 768


def _layernorm(x):
    mu = jnp.mean(x, axis=-1, keepdims=True)
    xc = x - mu
    var = jnp.mean(xc * xc, axis=-1, keepdims=True)
    return xc * lax.rsqrt(var + LN_EPS)


def _silu(z):
    return z * jax.nn.sigmoid(z)


def _rope(x, cos, sin_signed):
    lane = lax.broadcasted_iota(jnp.int32, x.shape, 1)
    partner = jnp.where((lane & 16) == 0, pltpu.roll(x, LANES - 16, 1), pltpu.roll(x, 16, 1))
    return x * cos + partner * sin_signed


def _mod_kernel(c_ref, w_ref, b_ref, o_ref):
    c = c_ref[...]
    a = _silu(c).astype(BF16)
    o_ref[0] = jnp.dot(a, w_ref[0].astype(BF16), preferred_element_type=F32) + b_ref[0]


def _modulation(cond, w_mod, b_mod):
    depth, d, n = w_mod.shape
    rows = cond.shape[0]
    return pl.pallas_call(
        _mod_kernel,
        grid=(depth, n // MOD_COL_TILE),
        in_specs=[
            pl.BlockSpec((rows, d), lambda l, j: (0, 0)),
            pl.BlockSpec((1, d, MOD_COL_TILE), lambda l, j: (l, 0, j)),
            pl.BlockSpec((1, 1, MOD_COL_TILE), lambda l, j: (l, 0, j)),
        ],
        out_specs=pl.BlockSpec((1, rows, MOD_COL_TILE), lambda l, j: (l, 0, j)),
        out_shape=jax.ShapeDtypeStruct((depth, rows, n), F32),
        compiler_params=pltpu.CompilerParams(dimension_semantics=("parallel", "parallel")),
        name="modulation",
    )(cond, w_mod, b_mod.reshape(depth, 1, n))


def _p1_kernel(latent, *refs):
    if latent:
        x_ref, mod_ref, w1_ref, dftc_ref, cos_ref, sin_ref, h_ref, kb_ref, vt_ref, u_ref, z_ref = refs
    else:
        x_ref, mod_ref, w1_ref, dftc_ref, h_ref, kb_ref, vt_ref, u_ref, z_ref, k_ref, v_ref = refs
    shift = mod_ref[0, 0, 0:1, :]
    scale = mod_ref[0, 0, 1:2, :]
    bb, t = x_ref.shape[:2]
    if bb > 1:
        units = [(i, slice(0, t)) for i in range(bb)]
    else:
        units = [(0, slice(0, t // 2)), (0, slice(t // 2, t))]

    def chain(bi, rows):
        hb = (_layernorm(x_ref[bi, rows, :]) * (1.0 + scale) + shift).astype(BF16)
        h_ref[bi, rows, :] = hb
        yield
        def proj(lo, hi):
            return jnp.dot(hb, w1_ref[0, :, lo:hi], preferred_element_type=F32)

        kv = proj(O_K, O_ZA)
        k = kv[:, 0:KV_W]
        v = kv[:, KV_W:2 * KV_W]
        if latent:
            k = _rope(k, cos_ref[rows, :], sin_ref[rows, :])
        else:
            k_ref[bi, rows, :] = k
            v_ref[bi, rows, :] = v
        kb_ref[bi, rows, :] = k.astype(BF16)
        vt_ref[bi, :, rows] = v.T.astype(BF16)
        cu = proj(O_CC, O_ZB)
        u_ref[bi, rows, :] = cu[:, :CONV_W] * cu[:, CONV_W:]
        fx = proj(O_FX, O_ZC)
        for g in range(FOU_GROUPS):
            fg = fx[:, g * FOU_GW:(g + 1) * FOU_GW].astype(BF16)
            zg = jnp.dot(fg, dftc_ref[...], preferred_element_type=F32)
            z_ref[bi, 0, rows, g * FOU_GW:(g + 1) * FOU_GW] = zg[:, :FOU_GW].astype(BF16)
            z_ref[bi, 1, rows, g * FOU_GW:(g + 1) * FOU_GW] = zg[:, FOU_GW:].astype(BF16)
        yield

    chains = [chain(bi, rows) for bi, rows in units]
    for _stage in range(2):
        for ch in chains:
            next(ch)


def _pass1(x, mod, mod_row, layer, w1, dftc, rope_tabs, latent):
    b, s, d = x.shape
    t = min(P1_TOKEN_TILE, s)
    bb = P1_TOKEN_TILE // t if mod_row is not None else 1
    assert s % t == 0 and b % bb == 0 and (bb == 1 or t == s)
    nt = s // t
    if mod_row is None:
        mod_idx = lambda i, j: (layer, i, 0, 0)
    else:
        mod_idx = lambda i, j: (layer, mod_row, 0, 0)
    in_specs = [
        pl.BlockSpec((bb, t, d), lambda i, j: (i, j, 0)),
        pl.BlockSpec((1, 1, 3, d), mod_idx),
        pl.BlockSpec((1, d, IN_W), lambda i, j: (layer, 0, 0)),
        pl.BlockSpec((FOU_GW, 2 * FOU_GW), lambda i, j: (0, 0)),
    ]
    args = [x, mod, w1, dftc]
    if latent:
        in_specs += [pl.BlockSpec((t, KV_W), lambda i, j: (j, 0))] * 2
        args += list(rope_tabs)
    out_shape = [
        jax.ShapeDtypeStruct((b, s, d), BF16),
        jax.ShapeDtypeStruct((b, s, KV_W), BF16),
        jax.ShapeDtypeStruct((b, KV_W, s), BF16),
        jax.ShapeDtypeStruct((b, s, CONV_W), F32),
        jax.ShapeDtypeStruct((b, 2, s, FOU_W), BF16),
    ]
    out_specs = [
        pl.BlockSpec((bb, t, d), lambda i, j: (i, j, 0)),
        pl.BlockSpec((bb, t, KV_W), lambda i, j: (i, j, 0)),
        pl.BlockSpec((bb, KV_W, t), lambda i, j: (i, 0, j)),
        pl.BlockSpec((bb, t, CONV_W), lambda i, j: (i, j, 0)),
        pl.BlockSpec((bb, 2, t, FOU_W), lambda i, j: (i, 0, j, 0)),
    ]
    if not latent:
        out_shape += [jax.ShapeDtypeStruct((b, s, KV_W), F32)] * 2
        out_specs += [pl.BlockSpec((bb, t, KV_W), lambda i, j: (i, j, 0))] * 2
    return pl.pallas_call(
        functools.partial(_p1_kernel, latent),
        grid=(b // bb, nt),
        in_specs=in_specs,
        out_specs=out_specs,
        out_shape=out_shape,
        compiler_params=pltpu.CompilerParams(
            dimension_semantics=("parallel", "parallel"),
            vmem_limit_bytes=V7X_VMEM_BYTES * 3 // 4),
        name="pass1_latent" if latent else "pass1_context",
    )(*args)


def _cache_kernel(k_ref, v_ref, kb_ref, vt_ref):
    kb_ref[0, 0] = k_ref[0, 0].astype(BF16)
    vt_ref[0, 0] = v_ref[0, 0].T.astype(BF16)


def _prepare_cache(cache_k, cache_v):
    b, depth, past, w = cache_k.shape
    spec = pl.BlockSpec((1, 1, past, w), lambda i, l: (i, l, 0, 0))
    return pl.pallas_call(
        _cache_kernel,
        grid=(b, depth),
        in_specs=[spec, spec],
        out_specs=[spec, pl.BlockSpec((1, 1, w, past), lambda i, l: (i, l, 0, 0))],
        out_shape=[jax.ShapeDtypeStruct((b, depth, past, w), BF16),
                   jax.ShapeDtypeStruct((b, depth, w, past), BF16)],
        compiler_params=pltpu.CompilerParams(dimension_semantics=("parallel", "parallel")),
        name="cache_prep",
    )(cache_k, cache_v)


def _dft_kernel(n, tm, z_ref, ch_ref, sh_ref, rev_ref, o_ref, e_ref, od_ref, ab_ref, zh_ref):
    h = n // 2
    blk = rev_ref.shape[0]
    nblk = n // blk
    mi = pl.program_id(1)
    inv = n ** -0.5
    rev = rev_ref[...]

    @pl.when(mi == 0)
    def _fold():
        r = lax.broadcasted_iota(jnp.int32, (blk, 1), 0)
        sgn = jnp.where((r & 1) == 0, 1.0, -1.0)
        asum = jnp.zeros((1, e_ref.shape[1]), F32)
        for i in range(h // blk):
            q = nblk - 1 - i
            if i == 0:
                pad = jnp.zeros((blk, e_ref.shape[1]), BF16)
                mir_r = jnp.concatenate([z_ref[0, q * blk:(q + 1) * blk, :], pad], axis=0)
                mir_i = jnp.concatenate([z_ref[0, n + q * blk:n + (q + 1) * blk, :], pad], axis=0)
            else:
                mir_r = z_ref[0, q * blk:(q + 2) * blk, :]
                mir_i = z_ref[0, n + q * blk:n + (q + 2) * blk, :]
            e = z_ref[0, i * blk:(i + 1) * blk, :].astype(F32) + jnp.dot(rev, mir_r, preferred_element_type=F32)
            o = (z_ref[0, n + i * blk:n + (i + 1) * blk, :].astype(F32)
                 - jnp.dot(rev, mir_i, preferred_element_type=F32))
            e_ref[i * blk:(i + 1) * blk, :] = e.astype(BF16)
            od_ref[i * blk:(i + 1) * blk, :] = o.astype(BF16)
            asum = asum + jnp.sum(e * sgn, axis=0, keepdims=True)
        zh = z_ref[0, h:h + 1, :].astype(F32) * inv
        zh_ref[...] = zh
        ab_ref[h:h + blk, :] = jnp.where(r == 0, asum * inv + zh, 0.0).astype(BF16)

    rows = pl.ds(pl.multiple_of(mi * tm, tm), tm)
    a = jnp.dot(ch_ref[rows, :], e_ref[...], preferred_element_type=F32)
    b = jnp.dot(sh_ref[rows, :], od_ref[...], preferred_element_type=F32)
    rr = lax.broadcasted_iota(jnp.int32, (tm, 1), 0)
    a = a + jnp.where((rr & 1) == 0, 1.0, -1.0) * zh_ref[...]
    o_ref[0, rows, :] = (a - b).astype(o_ref.dtype)
    ab_ref[rows, :] = (a + b).astype(BF16)

    @pl.when(mi == pl.num_programs(1) - 1)
    def _unfold():
        for p in range(h // blk):
            q = h // blk - p - 1
            win = ab_ref[q * blk:(q + 2) * blk, :]
            o_ref[0, h + p * blk:h + (p + 1) * blk, :] = jnp.dot(
                rev, win, preferred_element_type=F32).astype(o_ref.dtype)


def _position_dft(ch, sh, rev, z):
    b, s2, w = z.shape
    s = s2 // 2
    h = s // 2
    blk = rev.shape[0]
    assert h % blk == 0 and rev.shape[1] == 2 * blk
    tm = min(DFT_ROW_TILE, h)
    return pl.pallas_call(
        functools.partial(_dft_kernel, s, tm),
        grid=(b, h // tm),
        in_specs=[
            pl.BlockSpec((1, s2, w), lambda i, m: (i, 0, 0)),
            pl.BlockSpec((h, h), lambda i, m: (0, 0)),
            pl.BlockSpec((h, h), lambda i, m: (0, 0)),
            pl.BlockSpec((blk, 2 * blk), lambda i, m: (0, 0)),
        ],
        out_specs=pl.BlockSpec((1, s, w), lambda i, m: (i, 0, 0)),
        out_shape=jax.ShapeDtypeStruct((b, s, w), BF16),
        scratch_shapes=[pltpu.VMEM((h, w), BF16), pltpu.VMEM((h, w), BF16),
                        pltpu.VMEM((h + blk, w), BF16), pltpu.VMEM((1, w), F32)],
        compiler_params=pltpu.CompilerParams(
            dimension_semantics=("parallel", "arbitrary"),
            vmem_limit_bytes=V7X_VMEM_BYTES * 7 // 8),
        name="position_dft",
    )(z, ch, sh, rev)


def _scores_t(qpad, keys, masks):
    scores = []
    for kb, mk in zip(keys, masks):
        s = lax.dot_general(kb, qpad, (((1,), (1,)), ((), ())), preferred_element_type=F32)
        if mk is not None:
            s = jnp.where(mk, s, NEG)
        scores.append(s)
    return scores


def _softmax_pv_t(scores, vals_t, sink2):
    m = sink2
    for s in scores:
        m = jnp.maximum(m, jnp.max(s, axis=0, keepdims=True))
    acc = None
    for s, vt in zip(scores, vals_t):
        pv = jnp.dot(vt, jnp.exp2(s - m).astype(BF16), preferred_element_type=F32)
        acc = pv if acc is None else acc + pv
    den = acc[HEAD_DIM:HEAD_DIM + 1, :] + jnp.exp2(sink2 - m)
    return acc[:HEAD_DIM, :] / den


def _p2_kernel(latent, alpha, seq_len, nsub, *refs):
    if latent:
        (x_ref, h_ref, mod_ref, w2_ref, wb_ref, wo_ref, cw_ref, lng_ref, lnb_ref, sink_ref, yf_ref,
         um_ref, up_ref, un_ref, km_ref, kp_ref, kn_ref, vm_ref, vp_ref, vn_ref,
         kc_ref, vc_ref, cos_ref, sin_ref, y_ref) = refs
    else:
        (x_ref, h_ref, mod_ref, w2_ref, wb_ref, wo_ref, cw_ref, lng_ref, lnb_ref, sink_ref, yf_ref,
         um_ref, km_ref, vm_ref, y_ref) = refs
    t = pl.program_id(1)
    nt = pl.num_programs(1)
    tq = SUB_TILE
    tile = nsub * tq if latent else tq

    def window(main_ref, prev_ref, next_ref, lo, hi, axis):
        parts = []
        if lo < 0:
            parts.append(prev_ref[0])
            lo = 0
        inner = slice(lo, min(hi, tile))
        parts.append(main_ref[0, inner, :] if axis == 0 else main_ref[0, :, inner])
        if hi > tile:
            parts.append(next_ref[0])
        return parts[0] if len(parts) == 1 else jnp.concatenate(parts, axis=axis)

    def chain(sub):
        bi, r0 = (0, sub * tq) if latent else (sub, 0)
        rows = slice(r0, r0 + tq)
        hb = h_ref[bi, rows, :]

        def proj(off, width):
            return jnp.dot(hb, w2_ref[0, :, off:off + width], preferred_element_type=F32)

        q = proj(O_Q, ATT_W)
        if latent:
            keys = [window(km_ref, kp_ref, kn_ref, r0 - WINDOW, r0 + tq + WINDOW, 0), kc_ref[0, 0]]
            vals = [window(vm_ref, vp_ref, vn_ref, r0 - WINDOW, r0 + tq + WINDOW, 1), vc_ref[0, 0]]
            nloc = keys[0].shape[0]
            j = lax.broadcasted_iota(jnp.int32, (nloc, tq), 0)
            r = lax.broadcasted_iota(jnp.int32, (nloc, tq), 1)
            dlt = j - r
            kpos = j + (t * tile + r0 - WINDOW)
            band = (dlt >= 0) & (dlt <= 2 * WINDOW) & (kpos >= 0) & (kpos < seq_len)
            masks = [band, None]
            cos = cos_ref[rows, :]
            sin = sin_ref[rows, :]
        else:
            keys = [km_ref[bi]]
            vals = [vm_ref[bi]]
            masks = [None]
        vals_lo, vals_hi = [], []
        for vt in vals:
            ones = jnp.ones((2 * SUBLANES, vt.shape[1]), BF16)
            vals_lo.append(jnp.concatenate([vt[:HEAD_DIM], ones], axis=0))
            vals_hi.append(jnp.concatenate([vt[HEAD_DIM:], ones], axis=0))
        lane = lax.broadcasted_iota(jnp.int32, (tq, LANES), 1)
        low = lane < HEAD_DIM
        qblocks = []
        for b in range(ATT_W // LANES):
            qb = q[:, b * LANES:(b + 1) * LANES]
            if latent:
                qb = _rope(qb, cos, sin)
            qblocks.append(qb * (HEAD_DIM ** -0.5 * LOG2E))
        heads = []
        for g in range(Q_PER_KV):
            for kv in range(KV_HEADS):
                qb = qblocks[kv * (Q_PER_KV // 2) + g // 2]
                if g % 2 != kv:
                    qb = pltpu.roll(qb, HEAD_DIM, 1)
                qpad = jnp.where(low, qb, 0.0) if kv == 0 else jnp.where(low, 0.0, qb)
                h = kv * Q_PER_KV + g
                heads.append((qpad.astype(BF16), vals_lo if kv == 0 else vals_hi, sink_ref[h:h + 1, 0:1] * LOG2E))
        yield

        side_cols = [(O_ZA, ATT_W), (O_CB, CONV_W), (O_ZB, CONV_W), (O_ZC, FOU_W)]
        side_cols += [(O_G + br * D_MODEL, D_MODEL) for br in range(N_BRANCH)]
        side = []
        results = []
        sc = _scores_t(heads[0][0], keys, masks)
        for i, (_, vaug, sink2) in enumerate(heads):
            nxt = _scores_t(heads[i + 1][0], keys, masks) if i + 1 < len(heads) else None
            if i < len(side_cols):
                side.append(proj(*side_cols[i]))
            results.append(_softmax_pv_t(sc, vaug, sink2))
            sc = nxt
        p_za, p_cb, p_zb, p_zc = side[:4]
        p_gates = side[4:]
        out_t = {(i // KV_HEADS, i % KV_HEADS): o for i, o in enumerate(results)}
        ya = jnp.concatenate(
            [jnp.concatenate([out_t[(g, kv)], out_t[(g + 1, kv)]], axis=0).T
             for kv in range(KV_HEADS) for g in range(0, Q_PER_KV, 2)], axis=-1)
        y_a = ya * _silu(p_za)

        yield

        u = um_ref[bi, rows, :]
        if not latent:
            prev_row = next_row = jnp.zeros((1, CONV_W), F32)
        else:
            if r0 == 0:
                prev_row = jnp.where(t > 0, up_ref[0, SUBLANES - 1:SUBLANES, :], 0.0)
            else:
                prev_row = um_ref[0, r0 - 1:r0, :]
            if r0 + tq == tile:
                next_row = jnp.where(t < nt - 1, un_ref[0, 0:1, :], 0.0)
            else:
                next_row = um_ref[0, r0 + tq:r0 + tq + 1, :]
        row = lax.broadcasted_iota(jnp.int32, u.shape, 0)
        u_up = jnp.where(row == 0, prev_row, pltpu.roll(u, 1, 0))
        u_dn = jnp.where(row == tq - 1, next_row, pltpu.roll(u, tq - 1, 0))
        conv = u_up * cw_ref[0, 0:1, :] + u * cw_ref[0, 1:2, :] + u_dn * cw_ref[0, 2:3, :]
        y_b = p_cb * conv * _silu(p_zb)

        y_c = yf_ref[bi, rows, :].astype(F32) * _silu(p_zc)

        merged = None
        for br, yb in enumerate((y_a, y_b, y_c)):
            gt = jax.nn.sigmoid(p_gates[br])
            term = gt * jnp.dot(yb.astype(BF16), wb_ref[0, br], preferred_element_type=F32)
            merged = term if merged is None else merged + term
        out = jnp.dot(merged.astype(BF16), wo_ref[0], preferred_element_type=F32)

        yield

        res = alpha * x_ref[bi, rows, :] + mod_ref[0, 0, 2:3, :] * out
        y_ref[bi, rows, :] = _layernorm(res) * lng_ref[0] + lnb_ref[0]
        yield

    chains = [chain(sub) for sub in range(nsub)]
    for _stage in range(4):
        for ch in chains:
            next(ch)


def _pass2(x, hb, mod, mod_row, layer, w2, wb, wo, conv_w, ln_g, ln_b, sink_b, yf, u, k, v,
           cache_k, cache_v, rope_tabs, latent, alpha):
    b, s, d = x.shape
    nsub = P2_SUB_TILES
    if latent:
        bb, t = 1, nsub * SUB_TILE
        assert s % t == 0 and SUB_TILE == 2 * WINDOW
    else:
        bb, t = nsub, s
        assert s == SUB_TILE and b % nsub == 0
    nt = s // t
    if mod_row is None:
        mod_idx = lambda i, j: (layer, i, 0, 0)
    else:
        mod_idx = lambda i, j: (layer, mod_row, 0, 0)
    const3 = lambda i, j: (layer, 0, 0)
    tile3 = lambda i, j: (i, j, 0)
    in_specs = [
        pl.BlockSpec((bb, t, d), tile3),
        pl.BlockSpec((bb, t, d), tile3),
        pl.BlockSpec((1, 1, 3, d), mod_idx),
        pl.BlockSpec((1, d, IN_W), const3),
        pl.BlockSpec((1, N_BRANCH, BRANCH_W, d), lambda i, j: (layer, 0, 0, 0)),
        pl.BlockSpec((1, d, d), const3),
        pl.BlockSpec((1, 3, CONV_W), const3),
        pl.BlockSpec((1, 1, d), const3),
        pl.BlockSpec((1, 1, d), const3),
        pl.BlockSpec((N_HEADS, LANES), lambda i, j: (layer, 0)),
        pl.BlockSpec((bb, t, FOU_W), tile3),
        pl.BlockSpec((bb, t, CONV_W), tile3),
    ]
    args = [x, hb, mod, w2, wb, wo, conv_w, ln_g, ln_b, sink_b, yf, u]
    if latent:
        rpt = t // SUBLANES
        wpt = t // WINDOW
        nb8 = s // SUBLANES
        nbw = s // WINDOW
        prev8 = lambda i, j: (i, jnp.maximum(j * rpt - 1, 0), 0)
        next8 = lambda i, j: (i, jnp.minimum((j + 1) * rpt, nb8 - 1), 0)
        prevw = lambda i, j: (i, jnp.maximum(j * wpt - 1, 0), 0)
        nextw = lambda i, j: (i, jnp.minimum((j + 1) * wpt, nbw - 1), 0)
        in_specs += [pl.BlockSpec((1, SUBLANES, CONV_W), prev8), pl.BlockSpec((1, SUBLANES, CONV_W), next8)]
        args += [u, u]
        in_specs += [pl.BlockSpec((1, t, KV_W), tile3),
                     pl.BlockSpec((1, WINDOW, KV_W), prevw),
                     pl.BlockSpec((1, WINDOW, KV_W), nextw)]
        args += [k, k, k]
        swap = lambda f: (lambda i, j: (f(i, j)[0], 0, f(i, j)[1]))
        in_specs += [pl.BlockSpec((1, KV_W, t), swap(tile3)),
                     pl.BlockSpec((1, KV_W, WINDOW), swap(prevw)),
                     pl.BlockSpec((1, KV_W, WINDOW), swap(nextw))]
        args += [v, v, v]
        past = cache_k.shape[2]
        in_specs += [pl.BlockSpec((1, 1, past, KV_W), lambda i, j: (i, layer, 0, 0)),
                     pl.BlockSpec((1, 1, KV_W, past), lambda i, j: (i, layer, 0, 0))]
        args += [cache_k, cache_v]
        in_specs += [pl.BlockSpec((t, KV_W), lambda i, j: (j, 0))] * 2
        args += list(rope_tabs)
    else:
        in_specs += [pl.BlockSpec((bb, t, KV_W), tile3), pl.BlockSpec((bb, KV_W, t), lambda i, j: (i, 0, j))]
        args += [k, v]
    return pl.pallas_call(
        functools.partial(_p2_kernel, latent, alpha, s, nsub),
        grid=(b // bb, nt),
        in_specs=in_specs,
        out_specs=pl.BlockSpec((bb, t, d), tile3),
        out_shape=jax.ShapeDtypeStruct((b, s, d), F32),
        compiler_params=pltpu.CompilerParams(
            dimension_semantics=("parallel", "parallel"),
            vmem_limit_bytes=V7X_VMEM_BYTES * 7 // 8),
        name="pass2_latent" if latent else "pass2_context",
    )(*args)


def _dft_tables(n, m):
    idx = jnp.arange(m, dtype=jnp.int32)
    r = DFT_TABLE_FACTOR
    if m <= 4 * r or m % r:
        ang = (2.0 * np.pi / n) * ((idx[:, None] * idx[None, :]) % n).astype(F32)
        return jnp.cos(ang), jnp.sin(ang)
    hi = jnp.arange(m // r, dtype=jnp.int32)
    lo = jnp.arange(r, dtype=jnp.int32)
    a = (2.0 * np.pi / n) * ((r * hi[:, None] * idx[None, :]) % n).astype(F32)
    bb = (2.0 * np.pi / n) * ((lo[:, None] * idx[None, :]) % n).astype(F32)
    ca, sa = jnp.cos(a)[:, None, :], jnp.sin(a)[:, None, :]
    cb, sb = jnp.cos(bb)[None, :, :], jnp.sin(bb)[None, :, :]
    return (ca * cb - sa * sb).reshape(m, m), (sa * cb + ca * sb).reshape(m, m)


def _position_tables(s):
    c, sn = _dft_tables(s, s // 2)
    scale = s ** -0.5
    return (c * scale).astype(BF16), (sn * scale).astype(BF16)


def _reversal_matrix():
    r = np.arange(REV_BLOCK)[:, None]
    c = np.arange(2 * REV_BLOCK)[None, :]
    return jnp.asarray(c == REV_BLOCK - r, BF16)


def _channel_matrix():
    idx = np.arange(FOU_GW)
    ang = (2.0 * np.pi / FOU_GW) * ((idx[:, None] * idx[None, :]) % FOU_GW)
    tab = np.concatenate([np.cos(ang), np.sin(ang)], axis=1) * (FOU_GW ** -0.5)
    return jnp.asarray(tab, F32).astype(BF16)


def _rope_tables(n_tokens):
    pos = np.arange(n_tokens)
    row = (pos // GRID_W).astype(np.float64)
    col = (pos % GRID_W).astype(np.float64)
    n_freq = ROT_AXIS // 2
    inv_freq = ROPE_BASE ** (-np.arange(n_freq, dtype=np.float64) / n_freq)
    lane = np.arange(LANES)
    dd = lane % HEAD_DIM
    ang = np.where((dd < ROT_AXIS)[None, :], row[:, None], col[:, None]) * inv_freq[dd % n_freq][None, :]
    sign = np.where((lane % ROT_AXIS) < n_freq, -1.0, 1.0)
    return jnp.asarray(np.cos(ang), F32), jnp.asarray(np.sin(ang) * sign[None, :], F32)


def kernel(x_prompt, x_sample, cache_k, cache_v, c, c_ctx, w_mod, b_mod, w_in, sink, conv_w, w_branch, w_o, ln_g, ln_b):
    depth = w_in.shape[0]
    alpha = float((2 * depth) ** 0.25)
    dec_b, dec_s, d = x_sample.shape
    ctx_b, ctx_s, _ = x_prompt.shape

    rows = -(-(dec_b + 1) // SUBLANES) * SUBLANES
    cond = jnp.concatenate([c, c_ctx[None, :], jnp.zeros((rows - dec_b - 1, d), F32)], axis=0)
    mod = _modulation(cond, w_mod, b_mod).reshape(depth, rows, 3, d)

    w_all = w_in.astype(BF16)
    wb = w_branch.astype(BF16)
    wo = w_o.astype(BF16)
    sink_b = jnp.broadcast_to(sink.reshape(depth * N_HEADS, 1), (depth * N_HEADS, LANES))
    ln_g3 = ln_g.reshape(depth, 1, d)
    ln_b3 = ln_b.reshape(depth, 1, d)

    dftc = _channel_matrix()
    rev = _reversal_matrix()
    rope_tabs = _rope_tables(dec_s)
    past = cache_k.shape[2]
    ck, cv = _prepare_cache(cache_k.reshape(dec_b, depth, past, KV_W), cache_v.reshape(dec_b, depth, past, KV_W))

    def run_group(x, mod_row, latent):
        b, s, _ = x.shape
        ch, sh = _position_tables(s)
        y = x
        ks, vs = [], []
        for l in range(depth):
            hb, kb, vt, u, z, *kv32 = _pass1(y, mod, mod_row, l, w_all, dftc, rope_tabs, latent)
            yf = _position_dft(ch, sh, rev, z.reshape(b, 2 * s, FOU_W))
            y = _pass2(y, hb, mod, mod_row, l, w_all, wb, wo, conv_w, ln_g3, ln_b3, sink_b, yf, u, kb, vt,
                       ck, cv, rope_tabs, latent, alpha)
            if kv32:
                ks.append(kv32[0])
                vs.append(kv32[1])
        return y, ks, vs

    y_prompt, ks, vs = run_group(x_prompt, dec_b, False)
    new_k = jnp.stack(ks, axis=1).reshape(ctx_b, depth, ctx_s, KV_HEADS, HEAD_DIM)
    new_v = jnp.stack(vs, axis=1).reshape(ctx_b, depth, ctx_s, KV_HEADS, HEAD_DIM)
    y_sample, _, _ = run_group(x_sample, None, True)
    return (y_prompt, y_sample, new_k, new_v)
```

```python
import functools

import numpy as np
import jax
import jax.numpy as jnp
from jax import lax
from jax.experimental import pallas as pl
from jax.experimental.pallas import tpu as pltpu

F32 = jnp.float32
BF16 = jnp.bfloat16

D_MODEL = 1024
GRID_W = 64
N_HEADS = 8
KV_HEADS = 2
HEAD_DIM = 64
Q_PER_KV = N_HEADS // KV_HEADS
ATT_W = N_HEADS * HEAD_DIM
KV_W = KV_HEADS * HEAD_DIM
WINDOW = 128
CONV_W = 512
FOU_GROUPS = 4
FOU_GW = 128
FOU_W = FOU_GROUPS * FOU_GW
BRANCH_W = 512
N_BRANCH = 3
ROT_AXIS = HEAD_DIM // 2
ROPE_BASE = 10000.0
LN_EPS = 1e-6
NEG = -1e30
LOG2E = 1.4426950408889634

_OFF = np.cumsum([0, ATT_W, KV_W, KV_W, ATT_W, CONV_W, CONV_W, CONV_W, CONV_W, FOU_W, FOU_W, N_BRANCH * D_MODEL])
(O_Q, O_K, O_V, O_ZA, O_CB, O_CC, O_CX, O_ZB, O_FX, O_ZC, O_G, IN_W) = [int(v) for v in _OFF]

V7X_VMEM_BYTES = 64 * 1024 * 1024
LANES = 128
SUBLANES = 8

P1_TOKEN_TILE = 1024
SUB_TILE = 256
P2_SUB_TILES = 2
REV_BLOCK = 128
DFT_ROW_TILE = 512
DFT_TABLE_FACTOR = 64
MOD_COL_TILE = 768


def _layernorm(x):
    mu = jnp.mean(x, axis=-1, keepdims=True)
    xc = x - mu
    var = jnp.mean(xc * xc, axis=-1, keepdims=True)
    return xc * lax.rsqrt(var + LN_EPS)


def _silu(z):
    return z * jax.nn.sigmoid(z)


def _rope(x, cos, sin_signed):
    lane = lax.broadcasted_iota(jnp.int32, x.shape, 1)
    partner = jnp.where((lane & 16) == 0, pltpu.roll(x, LANES - 16, 1), pltpu.roll(x, 16, 1))
    return x * cos + partner * sin_signed


def _mod_kernel(c_ref, w_ref, b_ref, o_ref):
    c = c_ref[...]
    a = _silu(c).astype(BF16)
    o_ref[0] = jnp.dot(a, w_ref[0].astype(BF16), preferred_element_type=F32) + b_ref[0]


def _modulation(cond, w_mod, b_mod):
    depth, d, n = w_mod.shape
    rows = cond.shape[0]
    return pl.pallas_call(
        _mod_kernel,
        grid=(depth, n // MOD_COL_TILE),
        in_specs=[
            pl.BlockSpec((rows, d), lambda l, j: (0, 0)),
            pl.BlockSpec((1, d, MOD_COL_TILE), lambda l, j: (l, 0, j)),
            pl.BlockSpec((1, 1, MOD_COL_TILE), lambda l, j: (l, 0, j)),
        ],
        out_specs=pl.BlockSpec((1, rows, MOD_COL_TILE), lambda l, j: (l, 0, j)),
        out_shape=jax.ShapeDtypeStruct((depth, rows, n), F32),
        compiler_params=pltpu.CompilerParams(dimension_semantics=("parallel", "parallel")),
        name="modulation",
    )(cond, w_mod, b_mod.reshape(depth, 1, n))


def _p1_kernel(x_ref, mod_ref, w1_ref, dftc_ref, cos_ref, sin_ref, h_ref, kb_ref, vt_ref, u_ref, z_ref):
    shift = mod_ref[0, 0, 0:1, :]
    scale = mod_ref[0, 0, 1:2, :]
    t = x_ref.shape[1]

    def chain(rows):
        hb = (_layernorm(x_ref[0, rows, :]) * (1.0 + scale) + shift).astype(BF16)
        h_ref[0, rows, :] = hb
        yield

        def proj(lo, hi):
            return jnp.dot(hb, w1_ref[0, :, lo:hi], preferred_element_type=F32)

        kv = proj(O_K, O_ZA)
        k = _rope(kv[:, 0:KV_W], cos_ref[rows, :], sin_ref[rows, :])
        kb_ref[0, rows, :] = k.astype(BF16)
        vt_ref[0, :, rows] = kv[:, KV_W:2 * KV_W].T.astype(BF16)
        cu = proj(O_CC, O_ZB)
        u_ref[0, rows, :] = cu[:, :CONV_W] * cu[:, CONV_W:]
        fx = proj(O_FX, O_ZC)
        for g in range(FOU_GROUPS):
            fg = fx[:, g * FOU_GW:(g + 1) * FOU_GW].astype(BF16)
            zg = jnp.dot(fg, dftc_ref[...], preferred_element_type=F32)
            z_ref[0, 0, rows, g * FOU_GW:(g + 1) * FOU_GW] = zg[:, :FOU_GW].astype(BF16)
            z_ref[0, 1, rows, g * FOU_GW:(g + 1) * FOU_GW] = zg[:, FOU_GW:].astype(BF16)
        yield

    chains = [chain(slice(0, t // 2)), chain(slice(t // 2, t))]
    for _stage in range(2):
        for ch in chains:
            next(ch)


def _pass1(x, mod, layer, w1, dftc, rope_tabs):
    b, s, d = x.shape
    t = min(P1_TOKEN_TILE, s)
    assert s % t == 0
    nt = s // t
    tile3 = lambda i, j: (i, j, 0)
    in_specs = [
        pl.BlockSpec((1, t, d), tile3),
        pl.BlockSpec((1, 1, 3, d), lambda i, j: (layer, i, 0, 0)),
        pl.BlockSpec((1, d, IN_W), lambda i, j: (layer, 0, 0)),
        pl.BlockSpec((FOU_GW, 2 * FOU_GW), lambda i, j: (0, 0)),
        pl.BlockSpec((t, KV_W), lambda i, j: (j, 0)),
        pl.BlockSpec((t, KV_W), lambda i, j: (j, 0)),
    ]
    out_shape = [
        jax.ShapeDtypeStruct((b, s, d), BF16),
        jax.ShapeDtypeStruct((b, s, KV_W), BF16),
        jax.ShapeDtypeStruct((b, KV_W, s), BF16),
        jax.ShapeDtypeStruct((b, s, CONV_W), F32),
        jax.ShapeDtypeStruct((b, 2, s, FOU_W), BF16),
    ]
    out_specs = [
        pl.BlockSpec((1, t, d), tile3),
        pl.BlockSpec((1, t, KV_W), tile3),
        pl.BlockSpec((1, KV_W, t), lambda i, j: (i, 0, j)),
        pl.BlockSpec((1, t, CONV_W), tile3),
        pl.BlockSpec((1, 2, t, FOU_W), lambda i, j: (i, 0, j, 0)),
    ]
    return pl.pallas_call(
        _p1_kernel,
        grid=(b, nt),
        in_specs=in_specs,
        out_specs=out_specs,
        out_shape=out_shape,
        compiler_params=pltpu.CompilerParams(
            dimension_semantics=("parallel", "parallel"),
            vmem_limit_bytes=V7X_VMEM_BYTES * 3 // 4),
        name="pass1_latent",
    )(x, mod, w1, dftc, *rope_tabs)


def _cache_kernel(k_ref, v_ref, kb_ref, vt_ref):
    kb_ref[0, 0] = k_ref[0, 0].astype(BF16)
    vt_ref[0, 0] = v_ref[0, 0].T.astype(BF16)


def _prepare_cache(cache_k, cache_v):
    b, depth, past, w = cache_k.shape
    spec = pl.BlockSpec((1, 1, past, w), lambda i, l: (i, l, 0, 0))
    return pl.pallas_call(
        _cache_kernel,
        grid=(b, depth),
        in_specs=[spec, spec],
        out_specs=[spec, pl.BlockSpec((1, 1, w, past), lambda i, l: (i, l, 0, 0))],
        out_shape=[jax.ShapeDtypeStruct((b, depth, past, w), BF16),
                   jax.ShapeDtypeStruct((b, depth, w, past), BF16)],
        compiler_params=pltpu.CompilerParams(dimension_semantics=("parallel", "parallel")),
        name="cache_prep",
    )(cache_k, cache_v)


def _dft_kernel(n, tm, z_ref, ch_ref, sh_ref, rev_ref, o_ref, e_ref, od_ref, ab_ref, zh_ref):
    h = n // 2
    blk = rev_ref.shape[0]
    nblk = n // blk
    mi = pl.program_id(1)
    inv = n ** -0.5
    rev = rev_ref[...]

    @pl.when(mi == 0)
    def _fold():
        r = lax.broadcasted_iota(jnp.int32, (blk, 1), 0)
        sgn = jnp.where((r & 1) == 0, 1.0, -1.0)
        asum = jnp.zeros((1, e_ref.shape[1]), F32)
        for i in range(h // blk):
            q = nblk - 1 - i
            if i == 0:
                pad = jnp.zeros((blk, e_ref.shape[1]), BF16)
                mir_r = jnp.concatenate([z_ref[0, q * blk:(q + 1) * blk, :], pad], axis=0)
                mir_i = jnp.concatenate([z_ref[0, n + q * blk:n + (q + 1) * blk, :], pad], axis=0)
            else:
                mir_r = z_ref[0, q * blk:(q + 2) * blk, :]
                mir_i = z_ref[0, n + q * blk:n + (q + 2) * blk, :]
            e = z_ref[0, i * blk:(i + 1) * blk, :].astype(F32) + jnp.dot(rev, mir_r, preferred_element_type=F32)
            o = (z_ref[0, n + i * blk:n + (i + 1) * blk, :].astype(F32)
                 - jnp.dot(rev, mir_i, preferred_element_type=F32))
            e_ref[i * blk:(i + 1) * blk, :] = e.astype(BF16)
            od_ref[i * blk:(i + 1) * blk, :] = o.astype(BF16)
            asum = asum + jnp.sum(e * sgn, axis=0, keepdims=True)
        zh = z_ref[0, h:h + 1, :].astype(F32) * inv
        zh_ref[...] = zh
        ab_ref[h:h + blk, :] = jnp.where(r == 0, asum * inv + zh, 0.0).astype(BF16)

    rows = pl.ds(pl.multiple_of(mi * tm, tm), tm)
    a = jnp.dot(ch_ref[rows, :], e_ref[...], preferred_element_type=F32)
    b = jnp.dot(sh_ref[rows, :], od_ref[...], preferred_element_type=F32)
    rr = lax.broadcasted_iota(jnp.int32, (tm, 1), 0)
    a = a + jnp.where((rr & 1) == 0, 1.0, -1.0) * zh_ref[...]
    o_ref[0, rows, :] = (a - b).astype(o_ref.dtype)
    ab_ref[rows, :] = (a + b).astype(BF16)

    @pl.when(mi == pl.num_programs(1) - 1)
    def _unfold():
        for p in range(h // blk):
            q = h // blk - p - 1
            win = ab_ref[q * blk:(q + 2) * blk, :]
            o_ref[0, h + p * blk:h + (p + 1) * blk, :] = jnp.dot(
                rev, win, preferred_element_type=F32).astype(o_ref.dtype)


def _position_dft(ch, sh, rev, z):
    b, s2, w = z.shape
    s = s2 // 2
    h = s // 2
    blk = rev.shape[0]
    assert h % blk == 0 and rev.shape[1] == 2 * blk
    tm = min(DFT_ROW_TILE, h)
    return pl.pallas_call(
        functools.partial(_dft_kernel, s, tm),
        grid=(b, h // tm),
        in_specs=[
            pl.BlockSpec((1, s2, w), lambda i, m: (i, 0, 0)),
            pl.BlockSpec((h, h), lambda i, m: (0, 0)),
            pl.BlockSpec((h, h), lambda i, m: (0, 0)),
            pl.BlockSpec((blk, 2 * blk), lambda i, m: (0, 0)),
        ],
        out_specs=pl.BlockSpec((1, s, w), lambda i, m: (i, 0, 0)),
        out_shape=jax.ShapeDtypeStruct((b, s, w), BF16),
        scratch_shapes=[pltpu.VMEM((h, w), BF16), pltpu.VMEM((h, w), BF16),
                        pltpu.VMEM((h + blk, w), BF16), pltpu.VMEM((1, w), F32)],
        compiler_params=pltpu.CompilerParams(
            dimension_semantics=("parallel", "arbitrary"),
            vmem_limit_bytes=V7X_VMEM_BYTES * 7 // 8),
        name="position_dft",
    )(z, ch, sh, rev)


def _scores_t(qpad, keys, masks):
    scores = []
    for kb, mk in zip(keys, masks):
        s = lax.dot_general(kb, qpad, (((1,), (1,)), ((), ())), preferred_element_type=F32)
        if mk is not None:
            s = jnp.where(mk, s, NEG)
        scores.append(s)
    return scores


def _softmax_pv_t(scores, vals_t, sink2):
    m = sink2
    for s in scores:
        m = jnp.maximum(m, jnp.max(s, axis=0, keepdims=True))
    acc = None
    for s, vt in zip(scores, vals_t):
        pv = jnp.dot(vt, jnp.exp2(s - m).astype(BF16), preferred_element_type=F32)
        acc = pv if acc is None else acc + pv
    den = acc[HEAD_DIM:HEAD_DIM + 1, :] + jnp.exp2(sink2 - m)
    return acc[:HEAD_DIM, :] / den


def _p2_kernel(latent, alpha, seq_len, nsub, *refs):
    if latent:
        (x_ref, h_ref, mod_ref, w2_ref, wb_ref, wo_ref, cw_ref, lng_ref, lnb_ref, sink_ref, yf_ref,
         um_ref, up_ref, un_ref, km_ref, kp_ref, kn_ref, vm_ref, vp_ref, vn_ref,
         kc_ref, vc_ref, cos_ref, sin_ref, y_ref) = refs
    else:
        (x_ref, mod_ref, w2_ref, wb_ref, wo_ref, cw_ref, lng_ref, lnb_ref, sink_ref, dftc_ref, tpos_ref,
         y_ref, k_ref, v_ref) = refs
    t = pl.program_id(1)
    nt = pl.num_programs(1)
    tq = SUB_TILE
    tile = nsub * tq if latent else tq

    def window(main_ref, prev_ref, next_ref, lo, hi, axis):
        parts = []
        if lo < 0:
            parts.append(prev_ref[0])
            lo = 0
        inner = slice(lo, min(hi, tile))
        parts.append(main_ref[0, inner, :] if axis == 0 else main_ref[0, :, inner])
        if hi > tile:
            parts.append(next_ref[0])
        return parts[0] if len(parts) == 1 else jnp.concatenate(parts, axis=axis)

    def chain(sub):
        bi, r0 = (0, sub * tq) if latent else (sub, 0)
        rows = slice(r0, r0 + tq)
        if latent:
            hb = h_ref[bi, rows, :]
        else:
            shift = mod_ref[0, 0, 0:1, :]
            scale = mod_ref[0, 0, 1:2, :]
            hb = (_layernorm(x_ref[bi]) * (1.0 + scale) + shift).astype(BF16)

        def proj(off, width):
            return jnp.dot(hb, w2_ref[0, :, off:off + width], preferred_element_type=F32)

        q = proj(O_Q, ATT_W)
        if not latent:
            kv = proj(O_K, 2 * KV_W)
            k_ref[bi] = kv[:, :KV_W]
            v_ref[bi] = kv[:, KV_W:]
            keys = [kv[:, :KV_W].astype(BF16)]
            vals = [kv[:, KV_W:].T.astype(BF16)]
            masks = [None]
            cu = proj(O_CC, 2 * CONV_W)
            u_ctx = cu[:, :CONV_W] * cu[:, CONV_W:]
            fx = proj(O_FX, FOU_W)
            zs = [jnp.dot(fx[:, g * FOU_GW:(g + 1) * FOU_GW].astype(BF16), dftc_ref[...],
                          preferred_element_type=F32) for g in range(FOU_GROUPS)]
            zcat = jnp.concatenate([jnp.concatenate([z[:, :FOU_GW] for z in zs], axis=1),
                                    jnp.concatenate([z[:, FOU_GW:] for z in zs], axis=1)], axis=0)
            yf_ctx = jnp.dot(tpos_ref[...], zcat.astype(BF16), preferred_element_type=F32)
        if latent:
            keys = [window(km_ref, kp_ref, kn_ref, r0 - WINDOW, r0 + tq + WINDOW, 0), kc_ref[0, 0]]
            vals = [window(vm_ref, vp_ref, vn_ref, r0 - WINDOW, r0 + tq + WINDOW, 1), vc_ref[0, 0]]
            nloc = keys[0].shape[0]
            j = lax.broadcasted_iota(jnp.int32, (nloc, tq), 0)
            r = lax.broadcasted_iota(jnp.int32, (nloc, tq), 1)
            dlt = j - r
            kpos = j + (t * tile + r0 - WINDOW)
            band = (dlt >= 0) & (dlt <= 2 * WINDOW) & (kpos >= 0) & (kpos < seq_len)
            masks = [band, None]
            cos = cos_ref[rows, :]
            sin = sin_ref[rows, :]
        vals_lo, vals_hi = [], []
        for vt in vals:
            ones = jnp.ones((2 * SUBLANES, vt.shape[1]), BF16)
            vals_lo.append(jnp.concatenate([vt[:HEAD_DIM], ones], axis=0))
            vals_hi.append(jnp.concatenate([vt[HEAD_DIM:], ones], axis=0))
        lane = lax.broadcasted_iota(jnp.int32, (tq, LANES), 1)
        low = lane < HEAD_DIM
        qblocks = []
        for b in range(ATT_W // LANES):
            qb = q[:, b * LANES:(b + 1) * LANES]
            if latent:
                qb = _rope(qb, cos, sin)
            qblocks.append(qb * (HEAD_DIM ** -0.5 * LOG2E))
        heads = []
        for g in range(Q_PER_KV):
            for kv in range(KV_HEADS):
                qb = qblocks[kv * (Q_PER_KV // 2) + g // 2]
                if g % 2 != kv:
                    qb = pltpu.roll(qb, HEAD_DIM, 1)
                qpad = jnp.where(low, qb, 0.0) if kv == 0 else jnp.where(low, 0.0, qb)
                h = kv * Q_PER_KV + g
                heads.append((qpad.astype(BF16), vals_lo if kv == 0 else vals_hi, sink_ref[h:h + 1, 0:1] * LOG2E))
        yield

        side_cols = [(O_ZA, ATT_W), (O_CB, CONV_W), (O_ZB, CONV_W), (O_ZC, FOU_W)]
        side_cols += [(O_G + br * D_MODEL, D_MODEL) for br in range(N_BRANCH)]
        side = []
        results = []
        sc = _scores_t(heads[0][0], keys, masks)
        for i, (_, vaug, sink2) in enumerate(heads):
            nxt = _scores_t(heads[i + 1][0], keys, masks) if i + 1 < len(heads) else None
            if i < len(side_cols):
                side.append(proj(*side_cols[i]))
            results.append(_softmax_pv_t(sc, vaug, sink2))
            sc = nxt
        p_za, p_cb, p_zb, p_zc = side[:4]
        p_gates = side[4:]
        out_t = {(i // KV_HEADS, i % KV_HEADS): o for i, o in enumerate(results)}
        ya = jnp.concatenate(
            [jnp.concatenate([out_t[(g, kv)], out_t[(g + 1, kv)]], axis=0).T
             for kv in range(KV_HEADS) for g in range(0, Q_PER_KV, 2)], axis=-1)
        y_a = ya * _silu(p_za)

        yield

        if not latent:
            u = u_ctx
            prev_row = next_row = jnp.zeros((1, CONV_W), F32)
        else:
            u = um_ref[bi, rows, :]
            if r0 == 0:
                prev_row = jnp.where(t > 0, up_ref[0, SUBLANES - 1:SUBLANES, :], 0.0)
            else:
                prev_row = um_ref[0, r0 - 1:r0, :]
            if r0 + tq == tile:
                next_row = jnp.where(t < nt - 1, un_ref[0, 0:1, :], 0.0)
            else:
                next_row = um_ref[0, r0 + tq:r0 + tq + 1, :]
        row = lax.broadcasted_iota(jnp.int32, u.shape, 0)
        u_up = jnp.where(row == 0, prev_row, pltpu.roll(u, 1, 0))
        u_dn = jnp.where(row == tq - 1, next_row, pltpu.roll(u, tq - 1, 0))
        conv = u_up * cw_ref[0, 0:1, :] + u * cw_ref[0, 1:2, :] + u_dn * cw_ref[0, 2:3, :]
        y_b = p_cb * conv * _silu(p_zb)

        y_c = (yf_ref[bi, rows, :].astype(F32) if latent else yf_ctx) * _silu(p_zc)

        merged = None
        for br, yb in enumerate((y_a, y_b, y_c)):
            gt = jax.nn.sigmoid(p_gates[br])
            term = gt * jnp.dot(yb.astype(BF16), wb_ref[0, br], preferred_element_type=F32)
            merged = term if merged is None else merged + term
        out = jnp.dot(merged.astype(BF16), wo_ref[0], preferred_element_type=F32)

        yield

        res = alpha * x_ref[bi, rows, :] + mod_ref[0, 0, 2:3, :] * out
        y_ref[bi, rows, :] = _layernorm(res) * lng_ref[0] + lnb_ref[0]
        yield

    chains = [chain(sub) for sub in range(nsub)]
    for _stage in range(4):
        for ch in chains:
            next(ch)


def _pass2(x, hb, mod, layer, w2, wb, wo, conv_w, ln_g, ln_b, sink_b, yf, u, k, v, cache_k, cache_v, rope_tabs,
           alpha):
    b, s, d = x.shape
    nsub = P2_SUB_TILES
    t = nsub * SUB_TILE
    assert s % t == 0 and SUB_TILE == 2 * WINDOW
    nt = s // t
    const3 = lambda i, j: (layer, 0, 0)
    tile3 = lambda i, j: (i, j, 0)
    rpt = t // SUBLANES
    wpt = t // WINDOW
    nb8 = s // SUBLANES
    nbw = s // WINDOW
    prev8 = lambda i, j: (i, jnp.maximum(j * rpt - 1, 0), 0)
    next8 = lambda i, j: (i, jnp.minimum((j + 1) * rpt, nb8 - 1), 0)
    prevw = lambda i, j: (i, jnp.maximum(j * wpt - 1, 0), 0)
    nextw = lambda i, j: (i, jnp.minimum((j + 1) * wpt, nbw - 1), 0)
    swap = lambda f: (lambda i, j: (f(i, j)[0], 0, f(i, j)[1]))
    past = cache_k.shape[2]
    in_specs = [
        pl.BlockSpec((1, t, d), tile3),
        pl.BlockSpec((1, t, d), tile3),
        pl.BlockSpec((1, 1, 3, d), lambda i, j: (layer, i, 0, 0)),
        pl.BlockSpec((1, d, IN_W), const3),
        pl.BlockSpec((1, N_BRANCH, BRANCH_W, d), lambda i, j: (layer, 0, 0, 0)),
        pl.BlockSpec((1, d, d), const3),
        pl.BlockSpec((1, 3, CONV_W), const3),
        pl.BlockSpec((1, 1, d), const3),
        pl.BlockSpec((1, 1, d), const3),
        pl.BlockSpec((N_HEADS, LANES), lambda i, j: (layer, 0)),
        pl.BlockSpec((1, t, FOU_W), tile3),
        pl.BlockSpec((1, t, CONV_W), tile3),
        pl.BlockSpec((1, SUBLANES, CONV_W), prev8),
        pl.BlockSpec((1, SUBLANES, CONV_W), next8),
        pl.BlockSpec((1, t, KV_W), tile3),
        pl.BlockSpec((1, WINDOW, KV_W), prevw),
        pl.BlockSpec((1, WINDOW, KV_W), nextw),
        pl.BlockSpec((1, KV_W, t), swap(tile3)),
        pl.BlockSpec((1, KV_W, WINDOW), swap(prevw)),
        pl.BlockSpec((1, KV_W, WINDOW), swap(nextw)),
        pl.BlockSpec((1, 1, past, KV_W), lambda i, j: (i, layer, 0, 0)),
        pl.BlockSpec((1, 1, KV_W, past), lambda i, j: (i, layer, 0, 0)),
        pl.BlockSpec((t, KV_W), lambda i, j: (j, 0)),
        pl.BlockSpec((t, KV_W), lambda i, j: (j, 0)),
    ]
    args = [x, hb, mod, w2, wb, wo, conv_w, ln_g, ln_b, sink_b, yf, u, u, u, k, k, k, v, v, v,
            cache_k, cache_v, *rope_tabs]
    return pl.pallas_call(
        functools.partial(_p2_kernel, True, alpha, s, nsub),
        grid=(b, nt),
        in_specs=in_specs,
        out_specs=pl.BlockSpec((1, t, d), tile3),
        out_shape=jax.ShapeDtypeStruct((b, s, d), F32),
        compiler_params=pltpu.CompilerParams(
            dimension_semantics=("parallel", "parallel"),
            vmem_limit_bytes=V7X_VMEM_BYTES * 7 // 8),
        name="pass2_latent",
    )(*args)


def _context_layer(x, mod, mod_row, layer, w2, wb, wo, conv_w, ln_g, ln_b, sink_b, dftc, tpos, alpha):
    b, s, d = x.shape
    nsub = P2_SUB_TILES
    assert s == SUB_TILE and b % nsub == 0
    const3 = lambda i, j: (layer, 0, 0)
    tile3 = lambda i, j: (i, 0, 0)
    in_specs = [
        pl.BlockSpec((nsub, s, d), tile3),
        pl.BlockSpec((1, 1, 3, d), lambda i, j: (layer, mod_row, 0, 0)),
        pl.BlockSpec((1, d, IN_W), const3),
        pl.BlockSpec((1, N_BRANCH, BRANCH_W, d), lambda i, j: (layer, 0, 0, 0)),
        pl.BlockSpec((1, d, d), const3),
        pl.BlockSpec((1, 3, CONV_W), const3),
        pl.BlockSpec((1, 1, d), const3),
        pl.BlockSpec((1, 1, d), const3),
        pl.BlockSpec((N_HEADS, LANES), lambda i, j: (layer, 0)),
        pl.BlockSpec((FOU_GW, 2 * FOU_GW), lambda i, j: (0, 0)),
        pl.BlockSpec((s, 2 * s), lambda i, j: (0, 0)),
    ]
    return pl.pallas_call(
        functools.partial(_p2_kernel, False, alpha, s, nsub),
        grid=(b // nsub, 1),
        in_specs=in_specs,
        out_specs=[pl.BlockSpec((nsub, s, d), tile3), pl.BlockSpec((nsub, s, KV_W), tile3),
                   pl.BlockSpec((nsub, s, KV_W), tile3)],
        out_shape=[jax.ShapeDtypeStruct((b, s, d), F32), jax.ShapeDtypeStruct((b, s, KV_W), F32),
                   jax.ShapeDtypeStruct((b, s, KV_W), F32)],
        compiler_params=pltpu.CompilerParams(
            dimension_semantics=("parallel", "parallel"),
            vmem_limit_bytes=V7X_VMEM_BYTES * 7 // 8),
        name="context_layer",
    )(x, mod, w2, wb, wo, conv_w, ln_g, ln_b, sink_b, dftc, tpos)


def _dft_tables(n, m):
    idx = jnp.arange(m, dtype=jnp.int32)
    r = DFT_TABLE_FACTOR
    if m <= 4 * r or m % r:
        ang = (2.0 * np.pi / n) * ((idx[:, None] * idx[None, :]) % n).astype(F32)
        return jnp.cos(ang), jnp.sin(ang)
    hi = jnp.arange(m // r, dtype=jnp.int32)
    lo = jnp.arange(r, dtype=jnp.int32)
    a = (2.0 * np.pi / n) * ((r * hi[:, None] * idx[None, :]) % n).astype(F32)
    bb = (2.0 * np.pi / n) * ((lo[:, None] * idx[None, :]) % n).astype(F32)
    ca, sa = jnp.cos(a)[:, None, :], jnp.sin(a)[:, None, :]
    cb, sb = jnp.cos(bb)[None, :, :], jnp.sin(bb)[None, :, :]
    return (ca * cb - sa * sb).reshape(m, m), (sa * cb + ca * sb).reshape(m, m)


def _position_tables(s):
    c, sn = _dft_tables(s, s // 2)
    scale = s ** -0.5
    return (c * scale).astype(BF16), (sn * scale).astype(BF16)


def _reversal_matrix():
    r = np.arange(REV_BLOCK)[:, None]
    c = np.arange(2 * REV_BLOCK)[None, :]
    return jnp.asarray(c == REV_BLOCK - r, BF16)


def _channel_matrix():
    idx = np.arange(FOU_GW)
    ang = (2.0 * np.pi / FOU_GW) * ((idx[:, None] * idx[None, :]) % FOU_GW)
    tab = np.concatenate([np.cos(ang), np.sin(ang)], axis=1) * (FOU_GW ** -0.5)
    return jnp.asarray(tab, F32).astype(BF16)


def _rope_tables(n_tokens):
    pos = np.arange(n_tokens)
    row = (pos // GRID_W).astype(np.float64)
    col = (pos % GRID_W).astype(np.float64)
    n_freq = ROT_AXIS // 2
    inv_freq = ROPE_BASE ** (-np.arange(n_freq, dtype=np.float64) / n_freq)
    lane = np.arange(LANES)
    dd = lane % HEAD_DIM
    ang = np.where((dd < ROT_AXIS)[None, :], row[:, None], col[:, None]) * inv_freq[dd % n_freq][None, :]
    sign = np.where((lane % ROT_AXIS) < n_freq, -1.0, 1.0)
    return jnp.asarray(np.cos(ang), F32), jnp.asarray(np.sin(ang) * sign[None, :], F32)


def kernel(x_prompt, x_sample, cache_k, cache_v, c, c_ctx, w_mod, b_mod, w_in, sink, conv_w, w_branch, w_o, ln_g, ln_b):
    depth = w_in.shape[0]
    alpha = float((2 * depth) ** 0.25)
    dec_b, dec_s, d = x_sample.shape
    ctx_b, ctx_s, _ = x_prompt.shape

    rows = -(-(dec_b + 1) // SUBLANES) * SUBLANES
    cond = jnp.concatenate([c, c_ctx[None, :], jnp.zeros((rows - dec_b - 1, d), F32)], axis=0)
    mod = _modulation(cond, w_mod, b_mod).reshape(depth, rows, 3, d)

    w_all = w_in.astype(BF16)
    wb = w_branch.astype(BF16)
    wo = w_o.astype(BF16)
    sink_b = jnp.broadcast_to(sink.reshape(depth * N_HEADS, 1), (depth * N_HEADS, LANES))
    ln_g3 = ln_g.reshape(depth, 1, d)
    ln_b3 = ln_b.reshape(depth, 1, d)

    dftc = _channel_matrix()
    rev = _reversal_matrix()
    rope_tabs = _rope_tables(dec_s)
    past = cache_k.shape[2]
    ck, cv = _prepare_cache(cache_k.reshape(dec_b, depth, past, KV_W), cache_v.reshape(dec_b, depth, past, KV_W))

    c_ctx_t, s_ctx_t = _dft_tables(ctx_s, ctx_s)
    tpos = (jnp.concatenate([c_ctx_t, -s_ctx_t], axis=1) * (ctx_s ** -0.5)).astype(BF16)
    y_prompt = x_prompt
    ks, vs = [], []
    for l in range(depth):
        y_prompt, k_l, v_l = _context_layer(y_prompt, mod, dec_b, l, w_all, wb, wo, conv_w, ln_g3, ln_b3, sink_b,
                                            dftc, tpos, alpha)
        ks.append(k_l)
        vs.append(v_l)
    new_k = jnp.stack(ks, axis=1).reshape(ctx_b, depth, ctx_s, KV_HEADS, HEAD_DIM)
    new_v = jnp.stack(vs, axis=1).reshape(ctx_b, depth, ctx_s, KV_HEADS, HEAD_DIM)

    ch, sh = _position_tables(dec_s)
    y_sample = x_sample
    for l in range(depth):
        hb, kb, vt, u, z = _pass1(y_sample, mod, l, w_all, dftc, rope_tabs)
        yf = _position_dft(ch, sh, rev, z.reshape(dec_b, 2 * dec_s, FOU_W))
        y_sample = _pass2(y_sample, hb, mod, l, w_all, wb, wo, conv_w, ln_g3, ln_b3, sink_b, yf, u, kb, vt,
                          ck, cv, rope_tabs, alpha)
    return (y_prompt, y_sample, new_k, new_v)
```

```python
import functools

import numpy as np
import jax
import jax.numpy as jnp
from jax import lax
from jax.experimental import pallas as pl
from jax.experimental.pallas import tpu as pltpu

F32 = jnp.float32
BF16 = jnp.bfloat16

D_MODEL = 1024
GRID_W = 64
N_HEADS = 8
KV_HEADS = 2
HEAD_DIM = 64
Q_PER_KV = N_HEADS // KV_HEADS
ATT_W = N_HEADS * HEAD_DIM
KV_W = KV_HEADS * HEAD_DIM
WINDOW = 128
CONV_W = 512
FOU_GROUPS = 4
FOU_GW = 128
FOU_W = FOU_GROUPS * FOU_GW
BRANCH_W = 512
N_BRANCH = 3
ROT_AXIS = HEAD_DIM // 2
ROPE_BASE = 10000.0
LN_EPS = 1e-6
NEG = -1e30
LOG2E = 1.4426950408889634

_OFF = np.cumsum([0, ATT_W, KV_W, KV_W, ATT_W, CONV_W, CONV_W, CONV_W, CONV_W, FOU_W, FOU_W, N_BRANCH * D_MODEL])
(O_Q, O_K, O_V, O_ZA, O_CB, O_CC, O_CX, O_ZB, O_FX, O_ZC, O_G, IN_W) = [int(v) for v in _OFF]

V7X_VMEM_BYTES = 64 * 1024 * 1024
LANES = 128
SUBLANES = 8

P1_TOKEN_TILE = 1024
SUB_TILE = 256
P2_SUB_TILES = 2
REV_BLOCK = 128
DFT_ROW_TILE = 512
DFT_TABLE_FACTOR = 64
MOD_COL_TILE = 1536


def _layernorm(x):
    mu = jnp.mean(x, axis=-1, keepdims=True)
    xc = x - mu
    var = jnp.mean(xc * xc, axis=-1, keepdims=True)
    return xc * lax.rsqrt(var + LN_EPS)


def _silu(z):
    return z * jax.nn.sigmoid(z)


def _rope(x, cos, sin_signed):
    lane = lax.broadcasted_iota(jnp.int32, x.shape, 1)
    partner = jnp.where((lane & 16) == 0, pltpu.roll(x, LANES - 16, 1), pltpu.roll(x, 16, 1))
    return x * cos + partner * sin_signed


def _mod_kernel(c_ref, w_ref, b_ref, o_ref):
    c = c_ref[...]
    a = _silu(c).astype(BF16)
    o_ref[0] = jnp.dot(a, w_ref[0].astype(BF16), preferred_element_type=F32) + b_ref[0]


def _modulation(cond, w_mod, b_mod):
    depth, d, n = w_mod.shape
    rows = cond.shape[0]
    return pl.pallas_call(
        _mod_kernel,
        grid=(depth, n // MOD_COL_TILE),
        in_specs=[
            pl.BlockSpec((rows, d), lambda l, j: (0, 0)),
            pl.BlockSpec((1, d, MOD_COL_TILE), lambda l, j: (l, 0, j)),
            pl.BlockSpec((1, 1, MOD_COL_TILE), lambda l, j: (l, 0, j)),
        ],
        out_specs=pl.BlockSpec((1, rows, MOD_COL_TILE), lambda l, j: (l, 0, j)),
        out_shape=jax.ShapeDtypeStruct((depth, rows, n), F32),
        compiler_params=pltpu.CompilerParams(dimension_semantics=("parallel", "parallel")),
        name="modulation",
    )(cond, w_mod, b_mod.reshape(depth, 1, n))


def _p1_kernel(x_ref, mod_ref, w1_ref, dftc_ref, cos_ref, sin_ref, h_ref, kb_ref, vt_ref, u_ref, z_ref):
    shift = mod_ref[0, 0, 0:1, :]
    scale = mod_ref[0, 0, 1:2, :]
    t = x_ref.shape[1]

    def chain(rows):
        hb = (_layernorm(x_ref[0, rows, :]) * (1.0 + scale) + shift).astype(BF16)
        h_ref[0, rows, :] = hb
        yield

        def proj(lo, hi):
            return jnp.dot(hb, w1_ref[0, :, lo:hi], preferred_element_type=F32)

        kv = proj(O_K, O_ZA)
        k = _rope(kv[:, 0:KV_W], cos_ref[rows, :], sin_ref[rows, :])
        kb_ref[0, rows, :] = k.astype(BF16)
        vt_ref[0, :, rows] = kv[:, KV_W:2 * KV_W].T.astype(BF16)
        cu = proj(O_CC, O_ZB)
        u_ref[0, rows, :] = cu[:, :CONV_W] * cu[:, CONV_W:]
        fx = proj(O_FX, O_ZC)
        for g in range(FOU_GROUPS):
            fg = fx[:, g * FOU_GW:(g + 1) * FOU_GW].astype(BF16)
            zg = jnp.dot(fg, dftc_ref[...], preferred_element_type=F32)
            z_ref[0, 0, rows, g * FOU_GW:(g + 1) * FOU_GW] = zg[:, :FOU_GW].astype(BF16)
            z_ref[0, 1, rows, g * FOU_GW:(g + 1) * FOU_GW] = zg[:, FOU_GW:].astype(BF16)
        yield

    chains = [chain(slice(0, t // 2)), chain(slice(t // 2, t))]
    for _stage in range(2):
        for ch in chains:
            next(ch)


def _pass1(x, mod, layer, w1, dftc, rope_tabs):
    b, s, d = x.shape
    t = min(P1_TOKEN_TILE, s)
    assert s % t == 0
    nt = s // t
    tile3 = lambda i, j: (i, j, 0)
    in_specs = [
        pl.BlockSpec((1, t, d), tile3),
        pl.BlockSpec((1, 1, 3, d), lambda i, j: (layer, i, 0, 0)),
        pl.BlockSpec((1, d, IN_W), lambda i, j: (layer, 0, 0)),
        pl.BlockSpec((FOU_GW, 2 * FOU_GW), lambda i, j: (0, 0)),
        pl.BlockSpec((t, KV_W), lambda i, j: (j, 0)),
        pl.BlockSpec((t, KV_W), lambda i, j: (j, 0)),
    ]
    out_shape = [
        jax.ShapeDtypeStruct((b, s, d), BF16),
        jax.ShapeDtypeStruct((b, s, KV_W), BF16),
        jax.ShapeDtypeStruct((b, KV_W, s), BF16),
        jax.ShapeDtypeStruct((b, s, CONV_W), F32),
        jax.ShapeDtypeStruct((b, 2, s, FOU_W), BF16),
    ]
    out_specs = [
        pl.BlockSpec((1, t, d), tile3),
        pl.BlockSpec((1, t, KV_W), tile3),
        pl.BlockSpec((1, KV_W, t), lambda i, j: (i, 0, j)),
        pl.BlockSpec((1, t, CONV_W), tile3),
        pl.BlockSpec((1, 2, t, FOU_W), lambda i, j: (i, 0, j, 0)),
    ]
    return pl.pallas_call(
        _p1_kernel,
        grid=(b, nt),
        in_specs=in_specs,
        out_specs=out_specs,
        out_shape=out_shape,
        compiler_params=pltpu.CompilerParams(
            dimension_semantics=("parallel", "parallel"),
            vmem_limit_bytes=V7X_VMEM_BYTES * 3 // 4),
        name="pass1_latent",
    )(x, mod, w1, dftc, *rope_tabs)


def _cache_kernel(k_ref, v_ref, kb_ref, vt_ref):
    for i in range(k_ref.shape[0]):
        kb_ref[i, 0] = k_ref[i, 0].astype(BF16)
        vt_ref[i, 0] = v_ref[i, 0].T.astype(BF16)


def _prepare_cache(cache_k, cache_v):
    b, depth, past, w = cache_k.shape
    spec = pl.BlockSpec((b, 1, past, w), lambda l: (0, l, 0, 0))
    return pl.pallas_call(
        _cache_kernel,
        grid=(depth,),
        in_specs=[spec, spec],
        out_specs=[spec, pl.BlockSpec((b, 1, w, past), lambda l: (0, l, 0, 0))],
        out_shape=[jax.ShapeDtypeStruct((b, depth, past, w), BF16),
                   jax.ShapeDtypeStruct((b, depth, w, past), BF16)],
        compiler_params=pltpu.CompilerParams(dimension_semantics=("parallel",)),
        name="cache_prep",
    )(cache_k, cache_v)


def _dft_kernel(n, tm, z_ref, ch_ref, sh_ref, rev_ref, o_ref, e_ref, od_ref, ab_ref, zh_ref):
    h = n // 2
    blk = rev_ref.shape[0]
    nblk = n // blk
    mi = pl.program_id(1)
    inv = n ** -0.5
    rev = rev_ref[...]

    @pl.when(mi == 0)
    def _fold():
        r = lax.broadcasted_iota(jnp.int32, (blk, 1), 0)
        sgn = jnp.where((r & 1) == 0, 1.0, -1.0)
        asum = jnp.zeros((1, e_ref.shape[1]), F32)
        for i in range(h // blk):
            q = nblk - 1 - i
            if i == 0:
                pad = jnp.zeros((blk, e_ref.shape[1]), BF16)
                mir_r = jnp.concatenate([z_ref[0, q * blk:(q + 1) * blk, :], pad], axis=0)
                mir_i = jnp.concatenate([z_ref[0, n + q * blk:n + (q + 1) * blk, :], pad], axis=0)
            else:
                mir_r = z_ref[0, q * blk:(q + 2) * blk, :]
                mir_i = z_ref[0, n + q * blk:n + (q + 2) * blk, :]
            e = z_ref[0, i * blk:(i + 1) * blk, :].astype(F32) + jnp.dot(rev, mir_r, preferred_element_type=F32)
            o = (z_ref[0, n + i * blk:n + (i + 1) * blk, :].astype(F32)
                 - jnp.dot(rev, mir_i, preferred_element_type=F32))
            e_ref[i * blk:(i + 1) * blk, :] = e.astype(BF16)
            od_ref[i * blk:(i + 1) * blk, :] = o.astype(BF16)
            asum = asum + jnp.sum(e * sgn, axis=0, keepdims=True)
        zh = z_ref[0, h:h + 1, :].astype(F32) * inv
        zh_ref[...] = zh
        ab_ref[h:h + blk, :] = jnp.where(r == 0, asum * inv + zh, 0.0).astype(BF16)

    rows = pl.ds(pl.multiple_of(mi * tm, tm), tm)
    a = jnp.dot(ch_ref[rows, :], e_ref[...], preferred_element_type=F32)
    b = jnp.dot(sh_ref[rows, :], od_ref[...], preferred_element_type=F32)
    rr = lax.broadcasted_iota(jnp.int32, (tm, 1), 0)
    a = a + jnp.where((rr & 1) == 0, 1.0, -1.0) * zh_ref[...]
    o_ref[0, rows, :] = (a - b).astype(o_ref.dtype)
    ab_ref[rows, :] = (a + b).astype(BF16)

    @pl.when(mi == pl.num_programs(1) - 1)
    def _unfold():
        for p in range(h // blk):
            q = h // blk - p - 1
            win = ab_ref[q * blk:(q + 2) * blk, :]
            o_ref[0, h + p * blk:h + (p + 1) * blk, :] = jnp.dot(
                rev, win, preferred_element_type=F32).astype(o_ref.dtype)


def _position_dft(ch, sh, rev, z):
    b, s2, w = z.shape
    s = s2 // 2
    h = s // 2
    blk = rev.shape[0]
    assert h % blk == 0 and rev.shape[1] == 2 * blk
    tm = min(DFT_ROW_TILE, h)
    return pl.pallas_call(
        functools.partial(_dft_kernel, s, tm),
        grid=(b, h // tm),
        in_specs=[
            pl.BlockSpec((1, s2, w), lambda i, m: (i, 0, 0)),
            pl.BlockSpec((h, h), lambda i, m: (0, 0)),
            pl.BlockSpec((h, h), lambda i, m: (0, 0)),
            pl.BlockSpec((blk, 2 * blk), lambda i, m: (0, 0)),
        ],
        out_specs=pl.BlockSpec((1, s, w), lambda i, m: (i, 0, 0)),
        out_shape=jax.ShapeDtypeStruct((b, s, w), BF16),
        scratch_shapes=[pltpu.VMEM((h, w), BF16), pltpu.VMEM((h, w), BF16),
                        pltpu.VMEM((h + blk, w), BF16), pltpu.VMEM((1, w), F32)],
        compiler_params=pltpu.CompilerParams(
            dimension_semantics=("parallel", "arbitrary"),
            vmem_limit_bytes=V7X_VMEM_BYTES * 7 // 8),
        name="position_dft",
    )(z, ch, sh, rev)


def _scores_t(qpad, keys, masks):
    scores = []
    for kb, mk in zip(keys, masks):
        s = lax.dot_general(kb, qpad, (((1,), (1,)), ((), ())), preferred_element_type=F32)
        if mk is not None:
            s = jnp.where(mk, s, NEG)
        scores.append(s)
    return scores


def _softmax_pv_t(scores, vals_t, sink2):
    m = sink2
    for s in scores:
        m = jnp.maximum(m, jnp.max(s, axis=0, keepdims=True))
    acc = None
    for s, vt in zip(scores, vals_t):
        pv = jnp.dot(vt, jnp.exp2(s - m).astype(BF16), preferred_element_type=F32)
        acc = pv if acc is None else acc + pv
    den = acc[HEAD_DIM:HEAD_DIM + 1, :] + jnp.exp2(sink2 - m)
    return acc[:HEAD_DIM, :] / den


def _p2_kernel(latent, alpha, seq_len, nsub, *refs):
    if latent:
        (x_ref, h_ref, mod_ref, w2_ref, wb_ref, wo_ref, cw_ref, lng_ref, lnb_ref, sink_ref, yf_ref,
         um_ref, up_ref, un_ref, km_ref, kp_ref, kn_ref, vm_ref, vp_ref, vn_ref,
         kc_ref, vc_ref, cos_ref, sin_ref, y_ref) = refs
    else:
        (x_ref, mod_ref, w2_ref, wb_ref, wo_ref, cw_ref, lng_ref, lnb_ref, sink_ref, dftc_ref, tpos_ref,
         y_ref, k_ref, v_ref) = refs
    t = pl.program_id(1)
    nt = pl.num_programs(1)
    tq = SUB_TILE
    tile = nsub * tq if latent else tq

    def window(main_ref, prev_ref, next_ref, lo, hi, axis):
        parts = []
        if lo < 0:
            parts.append(prev_ref[0])
            lo = 0
        inner = slice(lo, min(hi, tile))
        parts.append(main_ref[0, inner, :] if axis == 0 else main_ref[0, :, inner])
        if hi > tile:
            parts.append(next_ref[0])
        return parts[0] if len(parts) == 1 else jnp.concatenate(parts, axis=axis)

    def chain(sub):
        bi, r0 = (0, sub * tq) if latent else (sub, 0)
        rows = slice(r0, r0 + tq)
        if latent:
            hb = h_ref[bi, rows, :]
        else:
            shift = mod_ref[0, 0, 0:1, :]
            scale = mod_ref[0, 0, 1:2, :]
            hb = (_layernorm(x_ref[bi]) * (1.0 + scale) + shift).astype(BF16)

        def proj(off, width):
            return jnp.dot(hb, w2_ref[0, :, off:off + width], preferred_element_type=F32)

        q = proj(O_Q, ATT_W)
        if not latent:
            kv = proj(O_K, 2 * KV_W)
            k_ref[bi] = kv[:, :KV_W]
            v_ref[bi] = kv[:, KV_W:]
            keys = [kv[:, :KV_W].astype(BF16)]
            vals = [kv[:, KV_W:].T.astype(BF16)]
            masks = [None]
            cu = proj(O_CC, 2 * CONV_W)
            u_ctx = cu[:, :CONV_W] * cu[:, CONV_W:]
            fx = proj(O_FX, FOU_W)
            zs = [jnp.dot(fx[:, g * FOU_GW:(g + 1) * FOU_GW].astype(BF16), dftc_ref[...],
                          preferred_element_type=F32) for g in range(FOU_GROUPS)]
            zcat = jnp.concatenate([jnp.concatenate([z[:, :FOU_GW] for z in zs], axis=1),
                                    jnp.concatenate([z[:, FOU_GW:] for z in zs], axis=1)], axis=0)
            yf_ctx = jnp.dot(tpos_ref[...], zcat.astype(BF16), preferred_element_type=F32)
        if latent:
            keys = [window(km_ref, kp_ref, kn_ref, r0 - WINDOW, r0 + tq + WINDOW, 0), kc_ref[0, 0]]
            vals = [window(vm_ref, vp_ref, vn_ref, r0 - WINDOW, r0 + tq + WINDOW, 1), vc_ref[0, 0]]
            nloc = keys[0].shape[0]
            j = lax.broadcasted_iota(jnp.int32, (nloc, tq), 0)
            r = lax.broadcasted_iota(jnp.int32, (nloc, tq), 1)
            dlt = j - r
            kpos = j + (t * tile + r0 - WINDOW)
            band = (dlt >= 0) & (dlt <= 2 * WINDOW) & (kpos >= 0) & (kpos < seq_len)
            masks = [band, None]
            cos = cos_ref[rows, :]
            sin = sin_ref[rows, :]
        vals_lo, vals_hi = [], []
        for vt in vals:
            ones = jnp.ones((2 * SUBLANES, vt.shape[1]), BF16)
            vals_lo.append(jnp.concatenate([vt[:HEAD_DIM], ones], axis=0))
            vals_hi.append(jnp.concatenate([vt[HEAD_DIM:], ones], axis=0))
        lane = lax.broadcasted_iota(jnp.int32, (tq, LANES), 1)
        low = lane < HEAD_DIM
        qblocks = []
        for b in range(ATT_W // LANES):
            qb = q[:, b * LANES:(b + 1) * LANES]
            if latent:
                qb = _rope(qb, cos, sin)
            qblocks.append(qb * (HEAD_DIM ** -0.5 * LOG2E))
        heads = []
        for g in range(Q_PER_KV):
            for kv in range(KV_HEADS):
                qb = qblocks[kv * (Q_PER_KV // 2) + g // 2]
                if g % 2 != kv:
                    qb = pltpu.roll(qb, HEAD_DIM, 1)
                qpad = jnp.where(low, qb, 0.0) if kv == 0 else jnp.where(low, 0.0, qb)
                h = kv * Q_PER_KV + g
                heads.append((qpad.astype(BF16), vals_lo if kv == 0 else vals_hi, sink_ref[h:h + 1, 0:1] * LOG2E))
        yield

        gate_w = N_BRANCH * D_MODEL // 4
        side_cols = [(O_ZA, ATT_W), (O_CB, CONV_W), (O_ZB, CONV_W), (O_ZC, FOU_W)]
        side_cols += [(O_G + i * gate_w, gate_w) for i in range(4)]
        side = []
        results = []
        sc = _scores_t(heads[0][0], keys, masks)
        for i, (_, vaug, sink2) in enumerate(heads):
            nxt = _scores_t(heads[i + 1][0], keys, masks) if i + 1 < len(heads) else None
            if i < len(side_cols):
                side.append(proj(*side_cols[i]))
            results.append(_softmax_pv_t(sc, vaug, sink2))
            sc = nxt
        p_za, p_cb, p_zb, p_zc = side[:4]
        g_all = jnp.concatenate(side[4:], axis=-1)
        p_gates = [g_all[:, br * D_MODEL:(br + 1) * D_MODEL] for br in range(N_BRANCH)]
        out_t = {(i // KV_HEADS, i % KV_HEADS): o for i, o in enumerate(results)}
        ya = jnp.concatenate(
            [jnp.concatenate([out_t[(g, kv)], out_t[(g + 1, kv)]], axis=0).T
             for kv in range(KV_HEADS) for g in range(0, Q_PER_KV, 2)], axis=-1)
        y_a = ya * _silu(p_za)

        yield

        if not latent:
            u = u_ctx
            prev_row = next_row = jnp.zeros((1, CONV_W), F32)
        else:
            u = um_ref[bi, rows, :]
            if r0 == 0:
                prev_row = jnp.where(t > 0, up_ref[0, SUBLANES - 1:SUBLANES, :], 0.0)
            else:
                prev_row = um_ref[0, r0 - 1:r0, :]
            if r0 + tq == tile:
                next_row = jnp.where(t < nt - 1, un_ref[0, 0:1, :], 0.0)
            else:
                next_row = um_ref[0, r0 + tq:r0 + tq + 1, :]
        row = lax.broadcasted_iota(jnp.int32, u.shape, 0)
        u_up = jnp.where(row == 0, prev_row, pltpu.roll(u, 1, 0))
        u_dn = jnp.where(row == tq - 1, next_row, pltpu.roll(u, tq - 1, 0))
        conv = u_up * cw_ref[0, 0:1, :] + u * cw_ref[0, 1:2, :] + u_dn * cw_ref[0, 2:3, :]
        y_b = p_cb * conv * _silu(p_zb)

        y_c = (yf_ref[bi, rows, :].astype(F32) if latent else yf_ctx) * _silu(p_zc)

        merged = None
        for br, yb in enumerate((y_a, y_b, y_c)):
            gt = jax.nn.sigmoid(p_gates[br])
            term = gt * jnp.dot(yb.astype(BF16), wb_ref[0, br], preferred_element_type=F32)
            merged = term if merged is None else merged + term
        out = jnp.dot(merged.astype(BF16), wo_ref[0], preferred_element_type=F32)

        yield

        res = alpha * x_ref[bi, rows, :] + mod_ref[0, 0, 2:3, :] * out
        y_ref[bi, rows, :] = _layernorm(res) * lng_ref[0] + lnb_ref[0]
        yield

    chains = [chain(sub) for sub in range(nsub)]
    for _stage in range(4):
        for ch in chains:
            next(ch)


def _pass2(x, hb, mod, layer, w2, wb, wo, conv_w, ln_g, ln_b, sink_b, yf, u, k, v, cache_k, cache_v, rope_tabs,
           alpha):
    b, s, d = x.shape
    nsub = P2_SUB_TILES
    t = nsub * SUB_TILE
    assert s % t == 0 and SUB_TILE == 2 * WINDOW
    nt = s // t
    const3 = lambda i, j: (layer, 0, 0)
    tile3 = lambda i, j: (i, j, 0)
    rpt = t // SUBLANES
    wpt = t // WINDOW
    nb8 = s // SUBLANES
    nbw = s // WINDOW
    prev8 = lambda i, j: (i, jnp.maximum(j * rpt - 1, 0), 0)
    next8 = lambda i, j: (i, jnp.minimum((j + 1) * rpt, nb8 - 1), 0)
    prevw = lambda i, j: (i, jnp.maximum(j * wpt - 1, 0), 0)
    nextw = lambda i, j: (i, jnp.minimum((j + 1) * wpt, nbw - 1), 0)
    swap = lambda f: (lambda i, j: (f(i, j)[0], 0, f(i, j)[1]))
    past = cache_k.shape[2]
    in_specs = [
        pl.BlockSpec((1, t, d), tile3),
        pl.BlockSpec((1, t, d), tile3),
        pl.BlockSpec((1, 1, 3, d), lambda i, j: (layer, i, 0, 0)),
        pl.BlockSpec((1, d, IN_W), const3),
        pl.BlockSpec((1, N_BRANCH, BRANCH_W, d), lambda i, j: (layer, 0, 0, 0)),
        pl.BlockSpec((1, d, d), const3),
        pl.BlockSpec((1, 3, CONV_W), const3),
        pl.BlockSpec((1, 1, d), const3),
        pl.BlockSpec((1, 1, d), const3),
        pl.BlockSpec((N_HEADS, LANES), lambda i, j: (layer, 0)),
        pl.BlockSpec((1, t, FOU_W), tile3),
        pl.BlockSpec((1, t, CONV_W), tile3),
        pl.BlockSpec((1, SUBLANES, CONV_W), prev8),
        pl.BlockSpec((1, SUBLANES, CONV_W), next8),
        pl.BlockSpec((1, t, KV_W), tile3),
        pl.BlockSpec((1, WINDOW, KV_W), prevw),
        pl.BlockSpec((1, WINDOW, KV_W), nextw),
        pl.BlockSpec((1, KV_W, t), swap(tile3)),
        pl.BlockSpec((1, KV_W, WINDOW), swap(prevw)),
        pl.BlockSpec((1, KV_W, WINDOW), swap(nextw)),
        pl.BlockSpec((1, 1, past, KV_W), lambda i, j: (i, layer, 0, 0)),
        pl.BlockSpec((1, 1, KV_W, past), lambda i, j: (i, layer, 0, 0)),
        pl.BlockSpec((t, KV_W), lambda i, j: (j, 0)),
        pl.BlockSpec((t, KV_W), lambda i, j: (j, 0)),
    ]
    args = [x, hb, mod, w2, wb, wo, conv_w, ln_g, ln_b, sink_b, yf, u, u, u, k, k, k, v, v, v,
            cache_k, cache_v, *rope_tabs]
    return pl.pallas_call(
        functools.partial(_p2_kernel, True, alpha, s, nsub),
        grid=(b, nt),
        in_specs=in_specs,
        out_specs=pl.BlockSpec((1, t, d), tile3),
        out_shape=jax.ShapeDtypeStruct((b, s, d), F32),
        compiler_params=pltpu.CompilerParams(
            dimension_semantics=("parallel", "parallel"),
            vmem_limit_bytes=V7X_VMEM_BYTES * 7 // 8),
        name="pass2_latent",
    )(*args)


def _context_layer(x, mod, mod_row, layer, w2, wb, wo, conv_w, ln_g, ln_b, sink_b, dftc, tpos, alpha):
    b, s, d = x.shape
    nsub = P2_SUB_TILES
    assert s == SUB_TILE and b % nsub == 0
    const3 = lambda i, j: (layer, 0, 0)
    tile3 = lambda i, j: (i, 0, 0)
    in_specs = [
        pl.BlockSpec((nsub, s, d), tile3),
        pl.BlockSpec((1, 1, 3, d), lambda i, j: (layer, mod_row, 0, 0)),
        pl.BlockSpec((1, d, IN_W), const3),
        pl.BlockSpec((1, N_BRANCH, BRANCH_W, d), lambda i, j: (layer, 0, 0, 0)),
        pl.BlockSpec((1, d, d), const3),
        pl.BlockSpec((1, 3, CONV_W), const3),
        pl.BlockSpec((1, 1, d), const3),
        pl.BlockSpec((1, 1, d), const3),
        pl.BlockSpec((N_HEADS, LANES), lambda i, j: (layer, 0)),
        pl.BlockSpec((FOU_GW, 2 * FOU_GW), lambda i, j: (0, 0)),
        pl.BlockSpec((s, 2 * s), lambda i, j: (0, 0)),
    ]
    return pl.pallas_call(
        functools.partial(_p2_kernel, False, alpha, s, nsub),
        grid=(b // nsub, 1),
        in_specs=in_specs,
        out_specs=[pl.BlockSpec((nsub, s, d), tile3), pl.BlockSpec((nsub, s, KV_W), tile3),
                   pl.BlockSpec((nsub, s, KV_W), tile3)],
        out_shape=[jax.ShapeDtypeStruct((b, s, d), F32), jax.ShapeDtypeStruct((b, s, KV_W), F32),
                   jax.ShapeDtypeStruct((b, s, KV_W), F32)],
        compiler_params=pltpu.CompilerParams(
            dimension_semantics=("parallel", "parallel"),
            vmem_limit_bytes=V7X_VMEM_BYTES * 7 // 8),
        name="context_layer",
    )(x, mod, w2, wb, wo, conv_w, ln_g, ln_b, sink_b, dftc, tpos)


def _dft_tables(n, m):
    idx = jnp.arange(m, dtype=jnp.int32)
    r = DFT_TABLE_FACTOR
    if m <= 4 * r or m % r:
        ang = (2.0 * np.pi / n) * ((idx[:, None] * idx[None, :]) % n).astype(F32)
        return jnp.cos(ang), jnp.sin(ang)
    hi = jnp.arange(m // r, dtype=jnp.int32)
    lo = jnp.arange(r, dtype=jnp.int32)
    a = (2.0 * np.pi / n) * ((r * hi[:, None] * idx[None, :]) % n).astype(F32)
    bb = (2.0 * np.pi / n) * ((lo[:, None] * idx[None, :]) % n).astype(F32)
    ca, sa = jnp.cos(a)[:, None, :], jnp.sin(a)[:, None, :]
    cb, sb = jnp.cos(bb)[None, :, :], jnp.sin(bb)[None, :, :]
    return (ca * cb - sa * sb).reshape(m, m), (sa * cb + ca * sb).reshape(m, m)


def _position_tables(s):
    c, sn = _dft_tables(s, s // 2)
    scale = s ** -0.5
    return (c * scale).astype(BF16), (sn * scale).astype(BF16)


def _reversal_matrix():
    r = np.arange(REV_BLOCK)[:, None]
    c = np.arange(2 * REV_BLOCK)[None, :]
    return jnp.asarray(c == REV_BLOCK - r, BF16)


def _channel_matrix():
    idx = np.arange(FOU_GW)
    ang = (2.0 * np.pi / FOU_GW) * ((idx[:, None] * idx[None, :]) % FOU_GW)
    tab = np.concatenate([np.cos(ang), np.sin(ang)], axis=1) * (FOU_GW ** -0.5)
    return jnp.asarray(tab, F32).astype(BF16)


def _rope_tables(n_tokens):
    pos = np.arange(n_tokens)
    row = (pos // GRID_W).astype(np.float64)
    col = (pos % GRID_W).astype(np.float64)
    n_freq = ROT_AXIS // 2
    inv_freq = ROPE_BASE ** (-np.arange(n_freq, dtype=np.float64) / n_freq)
    lane = np.arange(LANES)
    dd = lane % HEAD_DIM
    ang = np.where((dd < ROT_AXIS)[None, :], row[:, None], col[:, None]) * inv_freq[dd % n_freq][None, :]
    sign = np.where((lane % ROT_AXIS) < n_freq, -1.0, 1.0)
    return jnp.asarray(np.cos(ang), F32), jnp.asarray(np.sin(ang) * sign[None, :], F32)


def kernel(x_prompt, x_sample, cache_k, cache_v, c, c_ctx, w_mod, b_mod, w_in, sink, conv_w, w_branch, w_o, ln_g, ln_b):
    depth = w_in.shape[0]
    alpha = float((2 * depth) ** 0.25)
    dec_b, dec_s, d = x_sample.shape
    ctx_b, ctx_s, _ = x_prompt.shape

    rows = -(-(dec_b + 1) // SUBLANES) * SUBLANES
    cond = jnp.concatenate([c, c_ctx[None, :], jnp.zeros((rows - dec_b - 1, d), F32)], axis=0)
    mod = _modulation(cond, w_mod, b_mod).reshape(depth, rows, 3, d)

    w_all = w_in.astype(BF16)
    wb = w_branch.astype(BF16)
    wo = w_o.astype(BF16)
    sink_b = jnp.broadcast_to(sink.reshape(depth * N_HEADS, 1), (depth * N_HEADS, LANES))
    ln_g3 = ln_g.reshape(depth, 1, d)
    ln_b3 = ln_b.reshape(depth, 1, d)

    dftc = _channel_matrix()
    rev = _reversal_matrix()
    rope_tabs = _rope_tables(dec_s)
    past = cache_k.shape[2]
    ck, cv = _prepare_cache(cache_k.reshape(dec_b, depth, past, KV_W), cache_v.reshape(dec_b, depth, past, KV_W))

    c_ctx_t, s_ctx_t = _dft_tables(ctx_s, ctx_s)
    tpos = (jnp.concatenate([c_ctx_t, -s_ctx_t], axis=1) * (ctx_s ** -0.5)).astype(BF16)
    y_prompt = x_prompt
    ks, vs = [], []
    for l in range(depth):
        y_prompt, k_l, v_l = _context_layer(y_prompt, mod, dec_b, l, w_all, wb, wo, conv_w, ln_g3, ln_b3, sink_b,
                                            dftc, tpos, alpha)
        ks.append(k_l)
        vs.append(v_l)
    new_k = jnp.stack(ks, axis=1).reshape(ctx_b, depth, ctx_s, KV_HEADS, HEAD_DIM)
    new_v = jnp.stack(vs, axis=1).reshape(ctx_b, depth, ctx_s, KV_HEADS, HEAD_DIM)

    ch, sh = _position_tables(dec_s)
    y_sample = x_sample
    for l in range(depth):
        hb, kb, vt, u, z = _pass1(y_sample, mod, l, w_all, dftc, rope_tabs)
        yf = _position_dft(ch, sh, rev, z.reshape(dec_b, 2 * dec_s, FOU_W))
        y_sample = _pass2(y_sample, hb, mod, l, w_all, wb, wo, conv_w, ln_g3, ln_b3, sink_b, yf, u, kb, vt,
                          ck, cv, rope_tabs, alpha)
    return (y_prompt, y_sample, new_k, new_v)
```

```python
import functools

import numpy as np
import jax
import jax.numpy as jnp
from jax import lax
from jax.experimental import pallas as pl
from jax.experimental.pallas import tpu as pltpu

F32 = jnp.float32
BF16 = jnp.bfloat16

D_MODEL = 1024
GRID_W = 64
N_HEADS = 8
KV_HEADS = 2
HEAD_DIM = 64
Q_PER_KV = N_HEADS // KV_HEADS
ATT_W = N_HEADS * HEAD_DIM
KV_W = KV_HEADS * HEAD_DIM
WINDOW = 128
CONV_W = 512
FOU_GROUPS = 4
FOU_GW = 128
FOU_W = FOU_GROUPS * FOU_GW
BRANCH_W = 512
N_BRANCH = 3
ROT_AXIS = HEAD_DIM // 2
ROPE_BASE = 10000.0
LN_EPS = 1e-6
NEG = -1e30
LOG2E = 1.4426950408889634

_OFF = np.cumsum([0, ATT_W, KV_W, KV_W, ATT_W, CONV_W, CONV_W, CONV_W, CONV_W, FOU_W, FOU_W, N_BRANCH * D_MODEL])
(O_Q, O_K, O_V, O_ZA, O_CB, O_CC, O_CX, O_ZB, O_FX, O_ZC, O_G, IN_W) = [int(v) for v in _OFF]

V7X_VMEM_BYTES = 64 * 1024 * 1024
LANES = 128
SUBLANES = 8

P1_TOKEN_TILE = 1024
SUB_TILE = 256
P2_SUB_TILES = 2
REV_BLOCK = 128
DFT_ROW_TILE = 512
DFT_TABLE_FACTOR = 64
MOD_COL_TILE = 1536


def _layernorm(x):
    mu = jnp.mean(x, axis=-1, keepdims=True)
    xc = x - mu
    var = jnp.mean(xc * xc, axis=-1, keepdims=True)
    return xc * lax.rsqrt(var + LN_EPS)


def _silu(z):
    return z * jax.nn.sigmoid(z)


def _rope(x, cos, sin_signed):
    lane = lax.broadcasted_iota(jnp.int32, x.shape, 1)
    partner = jnp.where((lane & 16) == 0, pltpu.roll(x, LANES - 16, 1), pltpu.roll(x, 16, 1))
    return x * cos + partner * sin_signed


def _mod_kernel(c_ref, w_ref, b_ref, o_ref):
    c = c_ref[...]
    a = _silu(c).astype(BF16)
    o_ref[0] = jnp.dot(a, w_ref[0].astype(BF16), preferred_element_type=F32) + b_ref[0]


def _modulation(cond, w_mod, b_mod):
    depth, d, n = w_mod.shape
    rows = cond.shape[0]
    return pl.pallas_call(
        _mod_kernel,
        grid=(depth, n // MOD_COL_TILE),
        in_specs=[
            pl.BlockSpec((rows, d), lambda l, j: (0, 0)),
            pl.BlockSpec((1, d, MOD_COL_TILE), lambda l, j: (l, 0, j)),
            pl.BlockSpec((1, 1, MOD_COL_TILE), lambda l, j: (l, 0, j)),
        ],
        out_specs=pl.BlockSpec((1, rows, MOD_COL_TILE), lambda l, j: (l, 0, j)),
        out_shape=jax.ShapeDtypeStruct((depth, rows, n), F32),
        compiler_params=pltpu.CompilerParams(dimension_semantics=("parallel", "parallel")),
        name="modulation",
    )(cond, w_mod, b_mod.reshape(depth, 1, n))


def _p1_kernel(x_ref, mod_ref, w1_ref, dftc_ref, cos_ref, sin_ref, h_ref, kb_ref, vt_ref, u_ref, z_ref):
    shift = mod_ref[0, 0, 0:1, :]
    scale = mod_ref[0, 0, 1:2, :]
    t = x_ref.shape[1]

    def chain(rows):
        hb = (_layernorm(x_ref[0, rows, :]) * (1.0 + scale) + shift).astype(BF16)
        h_ref[0, rows, :] = hb
        yield

        def proj(lo, hi):
            return jnp.dot(hb, w1_ref[0, :, lo:hi], preferred_element_type=F32)

        kv = proj(O_K, O_ZA)
        k = _rope(kv[:, 0:KV_W], cos_ref[rows, :], sin_ref[rows, :])
        kb_ref[0, rows, :] = k.astype(BF16)
        vt_ref[0, :, rows] = kv[:, KV_W:2 * KV_W].T.astype(BF16)
        cu = proj(O_CC, O_ZB)
        u_ref[0, rows, :] = cu[:, :CONV_W] * cu[:, CONV_W:]
        fx = proj(O_FX, O_ZC)
        for g in range(FOU_GROUPS):
            fg = fx[:, g * FOU_GW:(g + 1) * FOU_GW].astype(BF16)
            zg = jnp.dot(fg, dftc_ref[...], preferred_element_type=F32)
            z_ref[0, 0, rows, g * FOU_GW:(g + 1) * FOU_GW] = zg[:, :FOU_GW].astype(BF16)
            z_ref[0, 1, rows, g * FOU_GW:(g + 1) * FOU_GW] = zg[:, FOU_GW:].astype(BF16)
        yield

    chains = [chain(slice(0, t // 2)), chain(slice(t // 2, t))]
    for _stage in range(2):
        for ch in chains:
            next(ch)


def _pass1(x, mod, layer, w1, dftc, rope_tabs):
    b, s, d = x.shape
    t = min(P1_TOKEN_TILE, s)
    assert s % t == 0
    nt = s // t
    tile3 = lambda i, j: (i, j, 0)
    in_specs = [
        pl.BlockSpec((1, t, d), tile3),
        pl.BlockSpec((1, 1, 3, d), lambda i, j: (layer, i, 0, 0)),
        pl.BlockSpec((1, d, IN_W), lambda i, j: (layer, 0, 0)),
        pl.BlockSpec((FOU_GW, 2 * FOU_GW), lambda i, j: (0, 0)),
        pl.BlockSpec((t, KV_W), lambda i, j: (j, 0)),
        pl.BlockSpec((t, KV_W), lambda i, j: (j, 0)),
    ]
    out_shape = [
        jax.ShapeDtypeStruct((b, s, d), BF16),
        jax.ShapeDtypeStruct((b, s, KV_W), BF16),
        jax.ShapeDtypeStruct((b, KV_W, s), BF16),
        jax.ShapeDtypeStruct((b, s, CONV_W), F32),
        jax.ShapeDtypeStruct((b, 2, s, FOU_W), BF16),
    ]
    out_specs = [
        pl.BlockSpec((1, t, d), tile3),
        pl.BlockSpec((1, t, KV_W), tile3),
        pl.BlockSpec((1, KV_W, t), lambda i, j: (i, 0, j)),
        pl.BlockSpec((1, t, CONV_W), tile3),
        pl.BlockSpec((1, 2, t, FOU_W), lambda i, j: (i, 0, j, 0)),
    ]
    return pl.pallas_call(
        _p1_kernel,
        grid=(b, nt),
        in_specs=in_specs,
        out_specs=out_specs,
        out_shape=out_shape,
        compiler_params=pltpu.CompilerParams(
            dimension_semantics=("parallel", "parallel"),
            vmem_limit_bytes=V7X_VMEM_BYTES * 3 // 4),
        name="pass1_latent",
    )(x, mod, w1, dftc, *rope_tabs)


def _cache_kernel(k_ref, v_ref, kb_ref, vt_ref):
    for i in range(k_ref.shape[0]):
        kb_ref[i, 0] = k_ref[i, 0].astype(BF16)
        vt_ref[i, 0] = v_ref[i, 0].T.astype(BF16)


def _prepare_cache(cache_k, cache_v):
    b, depth, past, w = cache_k.shape
    spec = pl.BlockSpec((b, 1, past, w), lambda l: (0, l, 0, 0))
    return pl.pallas_call(
        _cache_kernel,
        grid=(depth,),
        in_specs=[spec, spec],
        out_specs=[spec, pl.BlockSpec((b, 1, w, past), lambda l: (0, l, 0, 0))],
        out_shape=[jax.ShapeDtypeStruct((b, depth, past, w), BF16),
                   jax.ShapeDtypeStruct((b, depth, w, past), BF16)],
        compiler_params=pltpu.CompilerParams(dimension_semantics=("parallel",)),
        name="cache_prep",
    )(cache_k, cache_v)


def _dft_kernel(n, tm, nz, *refs):
    z_refs = refs[:nz]
    ch_ref, sh_ref, rev_ref, o_ref, e_ref, od_ref, ab_ref, zh_ref = refs[nz:]
    h = n // 2
    blk = rev_ref.shape[0]
    nblk = n // blk
    mi = pl.program_id(1)
    inv = n ** -0.5
    rev = rev_ref[...]
    zc = 2 * n // nz

    def zrows(a, b):
        assert a // zc == (b - 1) // zc
        return z_refs[a // zc][0, 0, a % zc:(b - 1) % zc + 1, :]

    @pl.when(mi == 0)
    def _fold():
        r = lax.broadcasted_iota(jnp.int32, (blk, 1), 0)
        sgn = jnp.where((r & 1) == 0, 1.0, -1.0)
        asum = jnp.zeros((1, e_ref.shape[1]), F32)
        for i in range(h // blk):
            q = nblk - 1 - i
            if i == 0:
                pad = jnp.zeros((blk, e_ref.shape[1]), BF16)
                mir_r = jnp.concatenate([zrows(q * blk, (q + 1) * blk), pad], axis=0)
                mir_i = jnp.concatenate([zrows(n + q * blk, n + (q + 1) * blk), pad], axis=0)
            else:
                mir_r = zrows(q * blk, (q + 2) * blk)
                mir_i = zrows(n + q * blk, n + (q + 2) * blk)
            e = zrows(i * blk, (i + 1) * blk).astype(F32) + jnp.dot(rev, mir_r, preferred_element_type=F32)
            o = (zrows(n + i * blk, n + (i + 1) * blk).astype(F32)
                 - jnp.dot(rev, mir_i, preferred_element_type=F32))
            e_ref[i * blk:(i + 1) * blk, :] = e.astype(BF16)
            od_ref[i * blk:(i + 1) * blk, :] = o.astype(BF16)
            asum = asum + jnp.sum(e * sgn, axis=0, keepdims=True)
        zh = zrows(h, h + 1).astype(F32) * inv
        zh_ref[...] = zh
        ab_ref[h:h + blk, :] = jnp.where(r == 0, asum * inv + zh, 0.0).astype(BF16)

    rows = pl.ds(pl.multiple_of(mi * tm, tm), tm)
    a = jnp.dot(ch_ref[rows, :], e_ref[...], preferred_element_type=F32)
    b = jnp.dot(sh_ref[rows, :], od_ref[...], preferred_element_type=F32)
    rr = lax.broadcasted_iota(jnp.int32, (tm, 1), 0)
    a = a + jnp.where((rr & 1) == 0, 1.0, -1.0) * zh_ref[...]
    o_ref[0, rows, :] = (a - b).astype(o_ref.dtype)
    ab_ref[rows, :] = (a + b).astype(BF16)

    @pl.when(mi == pl.num_programs(1) - 1)
    def _unfold():
        for p in range(h // blk):
            q = h // blk - p - 1
            win = ab_ref[q * blk:(q + 2) * blk, :]
            o_ref[0, h + p * blk:h + (p + 1) * blk, :] = jnp.dot(
                rev, win, preferred_element_type=F32).astype(o_ref.dtype)


def _position_dft(ch, sh, rev, z):
    b, s2, w = z.shape
    s = s2 // 2
    h = s // 2
    blk = rev.shape[0]
    assert h % blk == 0 and rev.shape[1] == 2 * blk
    tm = min(DFT_ROW_TILE, h)
    nz = h // tm
    zc = s2 // nz
    assert zc % (2 * blk) == 0 and (nz == 1 or s % zc == 0)
    z_specs = [pl.BlockSpec((1, 1, zc, w),
                            lambda i, m, q=q: (jnp.minimum(i + (m > q).astype(jnp.int32), b - 1), q, 0, 0))
               for q in range(nz)]
    z4 = z.reshape(b, nz, zc, w)
    return pl.pallas_call(
        functools.partial(_dft_kernel, s, tm, nz),
        grid=(b, nz),
        in_specs=z_specs + [
            pl.BlockSpec((h, h), lambda i, m: (0, 0)),
            pl.BlockSpec((h, h), lambda i, m: (0, 0)),
            pl.BlockSpec((blk, 2 * blk), lambda i, m: (0, 0)),
        ],
        out_specs=pl.BlockSpec((1, s, w), lambda i, m: (i, 0, 0)),
        out_shape=jax.ShapeDtypeStruct((b, s, w), BF16),
        scratch_shapes=[pltpu.VMEM((h, w), BF16), pltpu.VMEM((h, w), BF16),
                        pltpu.VMEM((h + blk, w), BF16), pltpu.VMEM((1, w), F32)],
        compiler_params=pltpu.CompilerParams(
            dimension_semantics=("parallel", "arbitrary"),
            vmem_limit_bytes=V7X_VMEM_BYTES * 7 // 8),
        name="position_dft",
    )(*([z4] * nz), ch, sh, rev)


def _scores_t(qpad, keys, masks):
    scores = []
    for kb, mk in zip(keys, masks):
        s = lax.dot_general(kb, qpad, (((1,), (1,)), ((), ())), preferred_element_type=F32)
        if mk is not None:
            s = jnp.where(mk, s, NEG)
        scores.append(s)
    return scores


def _softmax_pv_t(scores, vals_t, sink2):
    m = sink2
    for s in scores:
        m = jnp.maximum(m, jnp.max(s, axis=0, keepdims=True))
    acc = None
    for s, vt in zip(scores, vals_t):
        pv = jnp.dot(vt, jnp.exp2(s - m).astype(BF16), preferred_element_type=F32)
        acc = pv if acc is None else acc + pv
    den = acc[HEAD_DIM:HEAD_DIM + 1, :] + jnp.exp2(sink2 - m)
    return acc[:HEAD_DIM, :] / den


def _p2_kernel(latent, alpha, seq_len, nsub, *refs):
    if latent:
        (x_ref, h_ref, mod_ref, w2_ref, wb_ref, wo_ref, cw_ref, lng_ref, lnb_ref, sink_ref, yf_ref,
         um_ref, up_ref, un_ref, km_ref, kp_ref, kn_ref, vm_ref, vp_ref, vn_ref,
         kc_ref, vc_ref, cos_ref, sin_ref, y_ref) = refs
    else:
        (x_ref, mod_ref, w2_ref, wb_ref, wo_ref, cw_ref, lng_ref, lnb_ref, sink_ref, dftc_ref, tpos_ref,
         y_ref, k_ref, v_ref) = refs
    t = pl.program_id(1)
    nt = pl.num_programs(1)
    tq = SUB_TILE
    tile = nsub * tq if latent else tq

    def window(main_ref, prev_ref, next_ref, lo, hi, axis):
        parts = []
        if lo < 0:
            parts.append(prev_ref[0])
            lo = 0
        inner = slice(lo, min(hi, tile))
        parts.append(main_ref[0, inner, :] if axis == 0 else main_ref[0, :, inner])
        if hi > tile:
            parts.append(next_ref[0])
        return parts[0] if len(parts) == 1 else jnp.concatenate(parts, axis=axis)

    def chain(sub):
        bi, r0 = (0, sub * tq) if latent else (sub, 0)
        rows = slice(r0, r0 + tq)
        if latent:
            hb = h_ref[bi, rows, :]
        else:
            shift = mod_ref[0, 0, 0:1, :]
            scale = mod_ref[0, 0, 1:2, :]
            hb = (_layernorm(x_ref[bi]) * (1.0 + scale) + shift).astype(BF16)

        def proj(off, width):
            return jnp.dot(hb, w2_ref[0, :, off:off + width], preferred_element_type=F32)

        q = proj(O_Q, ATT_W)
        if not latent:
            kv = proj(O_K, 2 * KV_W)
            k_ref[bi] = kv[:, :KV_W]
            v_ref[bi] = kv[:, KV_W:]
            keys = [kv[:, :KV_W].astype(BF16)]
            vals = [kv[:, KV_W:].T.astype(BF16)]
            masks = [None]
            cu = proj(O_CC, 2 * CONV_W)
            u_ctx = cu[:, :CONV_W] * cu[:, CONV_W:]
            fx = proj(O_FX, FOU_W)
            zs = [jnp.dot(fx[:, g * FOU_GW:(g + 1) * FOU_GW].astype(BF16), dftc_ref[...],
                          preferred_element_type=F32) for g in range(FOU_GROUPS)]
            zcat = jnp.concatenate([jnp.concatenate([z[:, :FOU_GW] for z in zs], axis=1),
                                    jnp.concatenate([z[:, FOU_GW:] for z in zs], axis=1)], axis=0)
            yf_ctx = jnp.dot(tpos_ref[...], zcat.astype(BF16), preferred_element_type=F32)
        if latent:
            keys = [window(km_ref, kp_ref, kn_ref, r0 - WINDOW, r0 + tq + WINDOW, 0), kc_ref[0, 0]]
            vals = [window(vm_ref, vp_ref, vn_ref, r0 - WINDOW, r0 + tq + WINDOW, 1), vc_ref[0, 0]]
            nloc = keys[0].shape[0]
            j = lax.broadcasted_iota(jnp.int32, (nloc, tq), 0)
            r = lax.broadcasted_iota(jnp.int32, (nloc, tq), 1)
            dlt = j - r
            kpos = j + (t * tile + r0 - WINDOW)
            band = (dlt >= 0) & (dlt <= 2 * WINDOW) & (kpos >= 0) & (kpos < seq_len)
            masks = [band, None]
            cos = cos_ref[rows, :]
            sin = sin_ref[rows, :]
        vals_lo, vals_hi = [], []
        for vt in vals:
            ones = jnp.ones((2 * SUBLANES, vt.shape[1]), BF16)
            vals_lo.append(jnp.concatenate([vt[:HEAD_DIM], ones], axis=0))
            vals_hi.append(jnp.concatenate([vt[HEAD_DIM:], ones], axis=0))
        lane = lax.broadcasted_iota(jnp.int32, (tq, LANES), 1)
        low = lane < HEAD_DIM
        qblocks = []
        for b in range(ATT_W // LANES):
            qb = q[:, b * LANES:(b + 1) * LANES]
            if latent:
                qb = _rope(qb, cos, sin)
            qblocks.append(qb * (HEAD_DIM ** -0.5 * LOG2E))
        heads = []
        for g in range(Q_PER_KV):
            for kv in range(KV_HEADS):
                qb = qblocks[kv * (Q_PER_KV // 2) + g // 2]
                if g % 2 != kv:
                    qb = pltpu.roll(qb, HEAD_DIM, 1)
                qpad = jnp.where(low, qb, 0.0) if kv == 0 else jnp.where(low, 0.0, qb)
                h = kv * Q_PER_KV + g
                heads.append((qpad.astype(BF16), vals_lo if kv == 0 else vals_hi, sink_ref[h:h + 1, 0:1] * LOG2E))
        yield

        gate_w = N_BRANCH * D_MODEL // 4
        side_cols = [(O_ZA, ATT_W), (O_CB, CONV_W), (O_ZB, CONV_W), (O_ZC, FOU_W)]
        side_cols += [(O_G + i * gate_w, gate_w) for i in range(4)]
        side = []
        results = []
        sc = _scores_t(heads[0][0], keys, masks)
        for i, (_, vaug, sink2) in enumerate(heads):
            nxt = _scores_t(heads[i + 1][0], keys, masks) if i + 1 < len(heads) else None
            if i < len(side_cols):
                side.append(proj(*side_cols[i]))
            results.append(_softmax_pv_t(sc, vaug, sink2))
            sc = nxt
        p_za, p_cb, p_zb, p_zc = side[:4]
        g_all = jnp.concatenate(side[4:], axis=-1)
        p_gates = [g_all[:, br * D_MODEL:(br + 1) * D_MODEL] for br in range(N_BRANCH)]
        out_t = {(i // KV_HEADS, i % KV_HEADS): o for i, o in enumerate(results)}
        ya = jnp.concatenate(
            [jnp.concatenate([out_t[(g, kv)], out_t[(g + 1, kv)]], axis=0).T
             for kv in range(KV_HEADS) for g in range(0, Q_PER_KV, 2)], axis=-1)
        y_a = ya * _silu(p_za)

        yield

        if not latent:
            u = u_ctx
            prev_row = next_row = jnp.zeros((1, CONV_W), F32)
        else:
            u = um_ref[bi, rows, :]
            if r0 == 0:
                prev_row = jnp.where(t > 0, up_ref[0, SUBLANES - 1:SUBLANES, :], 0.0)
            else:
                prev_row = um_ref[0, r0 - 1:r0, :]
            if r0 + tq == tile:
                next_row = jnp.where(t < nt - 1, un_ref[0, 0:1, :], 0.0)
            else:
                next_row = um_ref[0, r0 + tq:r0 + tq + 1, :]
        row = lax.broadcasted_iota(jnp.int32, u.shape, 0)
        u_up = jnp.where(row == 0, prev_row, pltpu.roll(u, 1, 0))
        u_dn = jnp.where(row == tq - 1, next_row, pltpu.roll(u, tq - 1, 0))
        conv = u_up * cw_ref[0, 0:1, :] + u * cw_ref[0, 1:2, :] + u_dn * cw_ref[0, 2:3, :]
        y_b = p_cb * conv * _silu(p_zb)

        y_c = (yf_ref[bi, rows, :].astype(F32) if latent else yf_ctx) * _silu(p_zc)

        merged = None
        for br, yb in enumerate((y_a, y_b, y_c)):
            gt = jax.nn.sigmoid(p_gates[br])
            term = gt * jnp.dot(yb.astype(BF16), wb_ref[0, br], preferred_element_type=F32)
            merged = term if merged is None else merged + term
        out = jnp.dot(merged.astype(BF16), wo_ref[0], preferred_element_type=F32)

        yield

        res = alpha * x_ref[bi, rows, :] + mod_ref[0, 0, 2:3, :] * out
        y_ref[bi, rows, :] = _layernorm(res) * lng_ref[0] + lnb_ref[0]
        yield

    chains = [chain(sub) for sub in range(nsub)]
    for _stage in range(4):
        for ch in chains:
            next(ch)


def _pass2(x, hb, mod, layer, w2, wb, wo, conv_w, ln_g, ln_b, sink_b, yf, u, k, v, cache_k, cache_v, rope_tabs,
           alpha):
    b, s, d = x.shape
    nsub = P2_SUB_TILES
    t = nsub * SUB_TILE
    assert s % t == 0 and SUB_TILE == 2 * WINDOW
    nt = s // t
    const3 = lambda i, j: (layer, 0, 0)
    tile3 = lambda i, j: (i, j, 0)
    rpt = t // SUBLANES
    wpt = t // WINDOW
    nb8 = s // SUBLANES
    nbw = s // WINDOW
    prev8 = lambda i, j: (i, jnp.maximum(j * rpt - 1, 0), 0)
    next8 = lambda i, j: (i, jnp.minimum((j + 1) * rpt, nb8 - 1), 0)
    prevw = lambda i, j: (i, jnp.maximum(j * wpt - 1, 0), 0)
    nextw = lambda i, j: (i, jnp.minimum((j + 1) * wpt, nbw - 1), 0)
    swap = lambda f: (lambda i, j: (f(i, j)[0], 0, f(i, j)[1]))
    past = cache_k.shape[2]
    in_specs = [
        pl.BlockSpec((1, t, d), tile3),
        pl.BlockSpec((1, t, d), tile3),
        pl.BlockSpec((1, 1, 3, d), lambda i, j: (layer, i, 0, 0)),
        pl.BlockSpec((1, d, IN_W), const3),
        pl.BlockSpec((1, N_BRANCH, BRANCH_W, d), lambda i, j: (layer, 0, 0, 0)),
        pl.BlockSpec((1, d, d), const3),
        pl.BlockSpec((1, 3, CONV_W), const3),
        pl.BlockSpec((1, 1, d), const3),
        pl.BlockSpec((1, 1, d), const3),
        pl.BlockSpec((N_HEADS, LANES), lambda i, j: (layer, 0)),
        pl.BlockSpec((1, t, FOU_W), tile3),
        pl.BlockSpec((1, t, CONV_W), tile3),
        pl.BlockSpec((1, SUBLANES, CONV_W), prev8),
        pl.BlockSpec((1, SUBLANES, CONV_W), next8),
        pl.BlockSpec((1, t, KV_W), tile3),
        pl.BlockSpec((1, WINDOW, KV_W), prevw),
        pl.BlockSpec((1, WINDOW, KV_W), nextw),
        pl.BlockSpec((1, KV_W, t), swap(tile3)),
        pl.BlockSpec((1, KV_W, WINDOW), swap(prevw)),
        pl.BlockSpec((1, KV_W, WINDOW), swap(nextw)),
        pl.BlockSpec((1, 1, past, KV_W), lambda i, j: (i, layer, 0, 0)),
        pl.BlockSpec((1, 1, KV_W, past), lambda i, j: (i, layer, 0, 0)),
        pl.BlockSpec((t, KV_W), lambda i, j: (j, 0)),
        pl.BlockSpec((t, KV_W), lambda i, j: (j, 0)),
    ]
    args = [x, hb, mod, w2, wb, wo, conv_w, ln_g, ln_b, sink_b, yf, u, u, u, k, k, k, v, v, v,
            cache_k, cache_v, *rope_tabs]
    return pl.pallas_call(
        functools.partial(_p2_kernel, True, alpha, s, nsub),
        grid=(b, nt),
        in_specs=in_specs,
        out_specs=pl.BlockSpec((1, t, d), tile3),
        out_shape=jax.ShapeDtypeStruct((b, s, d), F32),
        compiler_params=pltpu.CompilerParams(
            dimension_semantics=("parallel", "parallel"),
            vmem_limit_bytes=V7X_VMEM_BYTES * 7 // 8),
        name="pass2_latent",
    )(*args)


def _context_layer(x, mod, mod_row, layer, w2, wb, wo, conv_w, ln_g, ln_b, sink_b, dftc, tpos, alpha):
    b, s, d = x.shape
    nsub = P2_SUB_TILES
    assert s == SUB_TILE and b % nsub == 0
    const3 = lambda i, j: (layer, 0, 0)
    tile3 = lambda i, j: (i, 0, 0)
    in_specs = [
        pl.BlockSpec((nsub, s, d), tile3),
        pl.BlockSpec((1, 1, 3, d), lambda i, j: (layer, mod_row, 0, 0)),
        pl.BlockSpec((1, d, IN_W), const3),
        pl.BlockSpec((1, N_BRANCH, BRANCH_W, d), lambda i, j: (layer, 0, 0, 0)),
        pl.BlockSpec((1, d, d), const3),
        pl.BlockSpec((1, 3, CONV_W), const3),
        pl.BlockSpec((1, 1, d), const3),
        pl.BlockSpec((1, 1, d), const3),
        pl.BlockSpec((N_HEADS, LANES), lambda i, j: (layer, 0)),
        pl.BlockSpec((FOU_GW, 2 * FOU_GW), lambda i, j: (0, 0)),
        pl.BlockSpec((s, 2 * s), lambda i, j: (0, 0)),
    ]
    return pl.pallas_call(
        functools.partial(_p2_kernel, False, alpha, s, nsub),
        grid=(b // nsub, 1),
        in_specs=in_specs,
        out_specs=[pl.BlockSpec((nsub, s, d), tile3), pl.BlockSpec((nsub, s, KV_W), tile3),
                   pl.BlockSpec((nsub, s, KV_W), tile3)],
        out_shape=[jax.ShapeDtypeStruct((b, s, d), F32), jax.ShapeDtypeStruct((b, s, KV_W), F32),
                   jax.ShapeDtypeStruct((b, s, KV_W), F32)],
        compiler_params=pltpu.CompilerParams(
            dimension_semantics=("parallel", "parallel"),
            vmem_limit_bytes=V7X_VMEM_BYTES * 7 // 8),
        name="context_layer",
    )(x, mod, w2, wb, wo, conv_w, ln_g, ln_b, sink_b, dftc, tpos)


def _dft_tables(n, m):
    idx = jnp.arange(m, dtype=jnp.int32)
    r = DFT_TABLE_FACTOR
    if m <= 4 * r or m % r:
        ang = (2.0 * np.pi / n) * ((idx[:, None] * idx[None, :]) % n).astype(F32)
        return jnp.cos(ang), jnp.sin(ang)
    hi = jnp.arange(m // r, dtype=jnp.int32)
    lo = jnp.arange(r, dtype=jnp.int32)
    a = (2.0 * np.pi / n) * ((r * hi[:, None] * idx[None, :]) % n).astype(F32)
    bb = (2.0 * np.pi / n) * ((lo[:, None] * idx[None, :]) % n).astype(F32)
    ca, sa = jnp.cos(a)[:, None, :], jnp.sin(a)[:, None, :]
    cb, sb = jnp.cos(bb)[None, :, :], jnp.sin(bb)[None, :, :]
    return (ca * cb - sa * sb).reshape(m, m), (sa * cb + ca * sb).reshape(m, m)


def _position_tables(s):
    c, sn = _dft_tables(s, s // 2)
    scale = s ** -0.5
    return (c * scale).astype(BF16), (sn * scale).astype(BF16)


def _reversal_matrix():
    r = np.arange(REV_BLOCK)[:, None]
    c = np.arange(2 * REV_BLOCK)[None, :]
    return jnp.asarray(c == REV_BLOCK - r, BF16)


def _channel_matrix():
    idx = np.arange(FOU_GW)
    ang = (2.0 * np.pi / FOU_GW) * ((idx[:, None] * idx[None, :]) % FOU_GW)
    tab = np.concatenate([np.cos(ang), np.sin(ang)], axis=1) * (FOU_GW ** -0.5)
    return jnp.asarray(tab, F32).astype(BF16)


def _rope_tables(n_tokens):
    pos = np.arange(n_tokens)
    row = (pos // GRID_W).astype(np.float64)
    col = (pos % GRID_W).astype(np.float64)
    n_freq = ROT_AXIS // 2
    inv_freq = ROPE_BASE ** (-np.arange(n_freq, dtype=np.float64) / n_freq)
    lane = np.arange(LANES)
    dd = lane % HEAD_DIM
    ang = np.where((dd < ROT_AXIS)[None, :], row[:, None], col[:, None]) * inv_freq[dd % n_freq][None, :]
    sign = np.where((lane % ROT_AXIS) < n_freq, -1.0, 1.0)
    return jnp.asarray(np.cos(ang), F32), jnp.asarray(np.sin(ang) * sign[None, :], F32)


def kernel(x_prompt, x_sample, cache_k, cache_v, c, c_ctx, w_mod, b_mod, w_in, sink, conv_w, w_branch, w_o, ln_g, ln_b):
    depth = w_in.shape[0]
    alpha = float((2 * depth) ** 0.25)
    dec_b, dec_s, d = x_sample.shape
    ctx_b, ctx_s, _ = x_prompt.shape

    rows = -(-(dec_b + 1) // SUBLANES) * SUBLANES
    cond = jnp.concatenate([c, c_ctx[None, :], jnp.zeros((rows - dec_b - 1, d), F32)], axis=0)
    mod = _modulation(cond, w_mod, b_mod).reshape(depth, rows, 3, d)

    w_all = w_in.astype(BF16)
    wb = w_branch.astype(BF16)
    wo = w_o.astype(BF16)
    sink_b = jnp.broadcast_to(sink.reshape(depth * N_HEADS, 1), (depth * N_HEADS, LANES))
    ln_g3 = ln_g.reshape(depth, 1, d)
    ln_b3 = ln_b.reshape(depth, 1, d)

    dftc = _channel_matrix()
    rev = _reversal_matrix()
    rope_tabs = _rope_tables(dec_s)
    past = cache_k.shape[2]
    ck, cv = _prepare_cache(cache_k.reshape(dec_b, depth, past, KV_W), cache_v.reshape(dec_b, depth, past, KV_W))

    c_ctx_t, s_ctx_t = _dft_tables(ctx_s, ctx_s)
    tpos = (jnp.concatenate([c_ctx_t, -s_ctx_t], axis=1) * (ctx_s ** -0.5)).astype(BF16)
    y_prompt = x_prompt
    ks, vs = [], []
    for l in range(depth):
        y_prompt, k_l, v_l = _context_layer(y_prompt, mod, dec_b, l, w_all, wb, wo, conv_w, ln_g3, ln_b3, sink_b,
                                            dftc, tpos, alpha)
        ks.append(k_l)
        vs.append(v_l)
    new_k = jnp.stack(ks, axis=1).reshape(ctx_b, depth, ctx_s, KV_HEADS, HEAD_DIM)
    new_v = jnp.stack(vs, axis=1).reshape(ctx_b, depth, ctx_s, KV_HEADS, HEAD_DIM)

    ch, sh = _position_tables(dec_s)
    y_sample = x_sample
    for l in range(depth):
        hb, kb, vt, u, z = _pass1(y_sample, mod, l, w_all, dftc, rope_tabs)
        yf = _position_dft(ch, sh, rev, z.reshape(dec_b, 2 * dec_s, FOU_W))
        y_sample = _pass2(y_sample, hb, mod, l, w_all, wb, wo, conv_w, ln_g3, ln_b3, sink_b, yf, u, kb, vt,
                          ck, cv, rope_tabs, alpha)
    return (y_prompt, y_sample, new_k, new_v)
```

```python
import functools

import numpy as np
import jax
import jax.numpy as jnp
from jax import lax
from jax.experimental import pallas as pl
from jax.experimental.pallas import tpu as pltpu

F32 = jnp.float32
BF16 = jnp.bfloat16

D_MODEL = 1024
GRID_W = 64
N_HEADS = 8
KV_HEADS = 2
HEAD_DIM = 64
Q_PER_KV = N_HEADS // KV_HEADS
ATT_W = N_HEADS * HEAD_DIM
KV_W = KV_HEADS * HEAD_DIM
WINDOW = 128
CONV_W = 512
FOU_GROUPS = 4
FOU_GW = 128
FOU_W = FOU_GROUPS * FOU_GW
BRANCH_W = 512
N_BRANCH = 3
ROT_AXIS = HEAD_DIM // 2
ROPE_BASE = 10000.0
LN_EPS = 1e-6
NEG = -1e30
LOG2E = 1.4426950408889634

_OFF = np.cumsum([0, ATT_W, KV_W, KV_W, ATT_W, CONV_W, CONV_W, CONV_W, CONV_W, FOU_W, FOU_W, N_BRANCH * D_MODEL])
(O_Q, O_K, O_V, O_ZA, O_CB, O_CC, O_CX, O_ZB, O_FX, O_ZC, O_G, IN_W) = [int(v) for v in _OFF]

V7X_VMEM_BYTES = 64 * 1024 * 1024
LANES = 128
SUBLANES = 8

P1_TOKEN_TILE = 1024
SUB_TILE = 256
P2_SUB_TILES = 2
REV_BLOCK = 128
DFT_ROW_TILE = 512
DFT_TABLE_FACTOR = 64
MOD_COL_TILE = 1536


def _layernorm(x):
    mu = jnp.mean(x, axis=-1, keepdims=True)
    xc = x - mu
    var = jnp.mean(xc * xc, axis=-1, keepdims=True)
    return xc * lax.rsqrt(var + LN_EPS)


def _silu(z):
    return z * jax.nn.sigmoid(z)


def _rope(x, cos, sin_signed):
    lane = lax.broadcasted_iota(jnp.int32, x.shape, 1)
    partner = jnp.where((lane & 16) == 0, pltpu.roll(x, LANES - 16, 1), pltpu.roll(x, 16, 1))
    return x * cos + partner * sin_signed


def _mod_kernel(c_ref, w_ref, b_ref, o_ref):
    c = c_ref[...]
    a = _silu(c).astype(BF16)
    o_ref[0] = jnp.dot(a, w_ref[0].astype(BF16), preferred_element_type=F32) + b_ref[0]


def _modulation(cond, w_mod, b_mod):
    depth, d, n = w_mod.shape
    rows = cond.shape[0]
    return pl.pallas_call(
        _mod_kernel,
        grid=(depth, n // MOD_COL_TILE),
        in_specs=[
            pl.BlockSpec((rows, d), lambda l, j: (0, 0)),
            pl.BlockSpec((1, d, MOD_COL_TILE), lambda l, j: (l, 0, j)),
            pl.BlockSpec((1, 1, MOD_COL_TILE), lambda l, j: (l, 0, j)),
        ],
        out_specs=pl.BlockSpec((1, rows, MOD_COL_TILE), lambda l, j: (l, 0, j)),
        out_shape=jax.ShapeDtypeStruct((depth, rows, n), F32),
        compiler_params=pltpu.CompilerParams(dimension_semantics=("parallel", "parallel")),
        name="modulation",
    )(cond, w_mod, b_mod.reshape(depth, 1, n))


def _p1_kernel(x_ref, mod_ref, w1_ref, dftc_ref, cos_ref, sin_ref, h_ref, kb_ref, vt_ref, u_ref, z_ref):
    shift = mod_ref[0, 0, 0:1, :]
    scale = mod_ref[0, 0, 1:2, :]
    t = x_ref.shape[1]

    def chain(rows):
        hb = (_layernorm(x_ref[0, rows, :]) * (1.0 + scale) + shift).astype(BF16)
        h_ref[0, rows, :] = hb
        yield

        def proj(lo, hi):
            return jnp.dot(hb, w1_ref[0, :, lo:hi], preferred_element_type=F32)

        kv = proj(O_K, O_ZA)
        k = _rope(kv[:, 0:KV_W], cos_ref[rows, :], sin_ref[rows, :])
        kb_ref[0, rows, :] = k.astype(BF16)
        vt_ref[0, :, rows] = kv[:, KV_W:2 * KV_W].T.astype(BF16)
        cu = proj(O_CC, O_ZB)
        u_ref[0, rows, :] = cu[:, :CONV_W] * cu[:, CONV_W:]
        fx = proj(O_FX, O_ZC)
        for g in range(FOU_GROUPS):
            fg = fx[:, g * FOU_GW:(g + 1) * FOU_GW].astype(BF16)
            zg = jnp.dot(fg, dftc_ref[...], preferred_element_type=F32)
            z_ref[0, 0, rows, g * FOU_GW:(g + 1) * FOU_GW] = zg[:, :FOU_GW].astype(BF16)
            z_ref[0, 1, rows, g * FOU_GW:(g + 1) * FOU_GW] = zg[:, FOU_GW:].astype(BF16)
        yield

    chains = [chain(slice(0, t // 2)), chain(slice(t // 2, t))]
    for _stage in range(2):
        for ch in chains:
            next(ch)


def _pass1(x, mod, layer, w1, dftc, rope_tabs):
    b, s, d = x.shape
    t = min(P1_TOKEN_TILE, s)
    assert s % t == 0
    nt = s // t
    tile3 = lambda i, j: (i, j, 0)
    in_specs = [
        pl.BlockSpec((1, t, d), tile3),
        pl.BlockSpec((1, 1, 3, d), lambda i, j: (layer, i, 0, 0)),
        pl.BlockSpec((1, d, IN_W), lambda i, j: (layer, 0, 0)),
        pl.BlockSpec((FOU_GW, 2 * FOU_GW), lambda i, j: (0, 0)),
        pl.BlockSpec((t, KV_W), lambda i, j: (j, 0)),
        pl.BlockSpec((t, KV_W), lambda i, j: (j, 0)),
    ]
    out_shape = [
        jax.ShapeDtypeStruct((b, s, d), BF16),
        jax.ShapeDtypeStruct((b, s, KV_W), BF16),
        jax.ShapeDtypeStruct((b, KV_W, s), BF16),
        jax.ShapeDtypeStruct((b, s, CONV_W), F32),
        jax.ShapeDtypeStruct((b, 2, s, FOU_W), BF16),
    ]
    out_specs = [
        pl.BlockSpec((1, t, d), tile3),
        pl.BlockSpec((1, t, KV_W), tile3),
        pl.BlockSpec((1, KV_W, t), lambda i, j: (i, 0, j)),
        pl.BlockSpec((1, t, CONV_W), tile3),
        pl.BlockSpec((1, 2, t, FOU_W), lambda i, j: (i, 0, j, 0)),
    ]
    return pl.pallas_call(
        _p1_kernel,
        grid=(b, nt),
        in_specs=in_specs,
        out_specs=out_specs,
        out_shape=out_shape,
        compiler_params=pltpu.CompilerParams(
            dimension_semantics=("parallel", "parallel"),
            vmem_limit_bytes=V7X_VMEM_BYTES * 3 // 4),
        name="pass1_latent",
    )(x, mod, w1, dftc, *rope_tabs)


def _cache_kernel(k_ref, v_ref, kb_ref, vt_ref):
    for i in range(k_ref.shape[0]):
        kb_ref[i, 0] = k_ref[i, 0].astype(BF16)
        vt_ref[i, 0] = v_ref[i, 0].T.astype(BF16)


def _prepare_cache(cache_k, cache_v):
    b, depth, past, w = cache_k.shape
    spec = pl.BlockSpec((b, 1, past, w), lambda l: (0, l, 0, 0))
    return pl.pallas_call(
        _cache_kernel,
        grid=(depth,),
        in_specs=[spec, spec],
        out_specs=[spec, pl.BlockSpec((b, 1, w, past), lambda l: (0, l, 0, 0))],
        out_shape=[jax.ShapeDtypeStruct((b, depth, past, w), BF16),
                   jax.ShapeDtypeStruct((b, depth, w, past), BF16)],
        compiler_params=pltpu.CompilerParams(dimension_semantics=("parallel",)),
        name="cache_prep",
    )(cache_k, cache_v)


def _dft_kernel(n, tm, z_ref, ch_ref, sh_ref, rev_ref, o_ref, e_ref, od_ref, ab_ref, zh_ref):
    h = n // 2
    blk = rev_ref.shape[0]
    nblk = n // blk
    mi = pl.program_id(1)
    inv = n ** -0.5
    rev = rev_ref[...]

    @pl.when(mi == 0)
    def _fold():
        r = lax.broadcasted_iota(jnp.int32, (blk, 1), 0)
        sgn = jnp.where((r & 1) == 0, 1.0, -1.0)
        asum = jnp.zeros((1, e_ref.shape[1]), F32)
        for i in range(h // blk):
            q = nblk - 1 - i
            if i == 0:
                pad = jnp.zeros((blk, e_ref.shape[1]), BF16)
                mir_r = jnp.concatenate([z_ref[0, q * blk:(q + 1) * blk, :], pad], axis=0)
                mir_i = jnp.concatenate([z_ref[0, n + q * blk:n + (q + 1) * blk, :], pad], axis=0)
            else:
                mir_r = z_ref[0, q * blk:(q + 2) * blk, :]
                mir_i = z_ref[0, n + q * blk:n + (q + 2) * blk, :]
            e = z_ref[0, i * blk:(i + 1) * blk, :].astype(F32) + jnp.dot(rev, mir_r, preferred_element_type=F32)
            o = (z_ref[0, n + i * blk:n + (i + 1) * blk, :].astype(F32)
                 - jnp.dot(rev, mir_i, preferred_element_type=F32))
            e_ref[i * blk:(i + 1) * blk, :] = e.astype(BF16)
            od_ref[i * blk:(i + 1) * blk, :] = o.astype(BF16)
            asum = asum + jnp.sum(e * sgn, axis=0, keepdims=True)
        zh = z_ref[0, h:h + 1, :].astype(F32) * inv
        zh_ref[...] = zh
        ab_ref[h:h + blk, :] = jnp.where(r == 0, asum * inv + zh, 0.0).astype(BF16)

    rows = pl.ds(pl.multiple_of(mi * tm, tm), tm)
    a = jnp.dot(ch_ref[rows, :], e_ref[...], preferred_element_type=F32)
    b = jnp.dot(sh_ref[rows, :], od_ref[...], preferred_element_type=F32)
    rr = lax.broadcasted_iota(jnp.int32, (tm, 1), 0)
    a = a + jnp.where((rr & 1) == 0, 1.0, -1.0) * zh_ref[...]
    o_ref[0, rows, :] = (a - b).astype(o_ref.dtype)
    ab_ref[rows, :] = (a + b).astype(BF16)

    @pl.when(mi == pl.num_programs(1) - 1)
    def _unfold():
        for p in range(h // blk):
            q = h // blk - p - 1
            win = ab_ref[q * blk:(q + 2) * blk, :]
            o_ref[0, h + p * blk:h + (p + 1) * blk, :] = jnp.dot(
                rev, win, preferred_element_type=F32).astype(o_ref.dtype)


def _position_dft(ch, sh, rev, z):
    b, s2, w = z.shape
    s = s2 // 2
    h = s // 2
    blk = rev.shape[0]
    assert h % blk == 0 and rev.shape[1] == 2 * blk
    tm = min(DFT_ROW_TILE, h)
    return pl.pallas_call(
        functools.partial(_dft_kernel, s, tm),
        grid=(b, h // tm),
        in_specs=[
            pl.BlockSpec((1, s2, w), lambda i, m: (i, 0, 0)),
            pl.BlockSpec((h, h), lambda i, m: (0, 0)),
            pl.BlockSpec((h, h), lambda i, m: (0, 0)),
            pl.BlockSpec((blk, 2 * blk), lambda i, m: (0, 0)),
        ],
        out_specs=pl.BlockSpec((1, s, w), lambda i, m: (i, 0, 0)),
        out_shape=jax.ShapeDtypeStruct((b, s, w), BF16),
        scratch_shapes=[pltpu.VMEM((h, w), BF16), pltpu.VMEM((h, w), BF16),
                        pltpu.VMEM((h + blk, w), BF16), pltpu.VMEM((1, w), F32)],
        compiler_params=pltpu.CompilerParams(
            dimension_semantics=("parallel", "arbitrary"),
            vmem_limit_bytes=V7X_VMEM_BYTES * 7 // 8),
        name="position_dft",
    )(z, ch, sh, rev)


def _scores_t(qpad, keys, masks):
    scores = []
    for kb, mk in zip(keys, masks):
        s = lax.dot_general(kb, qpad, (((1,), (1,)), ((), ())), preferred_element_type=F32)
        if mk is not None:
            s = jnp.where(mk, s, NEG)
        scores.append(s)
    return scores


def _softmax_pv_t(scores, vals_t, sink2):
    m = sink2
    for s in scores:
        m = jnp.maximum(m, jnp.max(s, axis=0, keepdims=True))
    acc = None
    for s, vt in zip(scores, vals_t):
        pv = jnp.dot(vt, jnp.exp2(s - m).astype(BF16), preferred_element_type=F32)
        acc = pv if acc is None else acc + pv
    den = acc[HEAD_DIM:HEAD_DIM + 1, :] + jnp.exp2(sink2 - m)
    return acc[:HEAD_DIM, :] / den


def _p2_kernel(latent, alpha, seq_len, nsub, *refs):
    if latent:
        (x_ref, h_ref, mod_ref, w2_ref, wb_ref, wo_ref, cw_ref, lng_ref, lnb_ref, sink_ref, yf_ref,
         um_ref, up_ref, un_ref, km_ref, kp_ref, kn_ref, vm_ref, vp_ref, vn_ref,
         kc_ref, vc_ref, cos_ref, sin_ref, y_ref) = refs
    else:
        (x_ref, mod_ref, w2_ref, wb_ref, wo_ref, cw_ref, lng_ref, lnb_ref, sink_ref, dftc_ref, tpos_ref,
         y_ref, k_ref, v_ref) = refs
    t = pl.program_id(1)
    nt = pl.num_programs(1)
    tq = SUB_TILE
    tile = nsub * tq if latent else tq

    def window(main_ref, prev_ref, next_ref, lo, hi, axis):
        parts = []
        if lo < 0:
            parts.append(prev_ref[0])
            lo = 0
        inner = slice(lo, min(hi, tile))
        parts.append(main_ref[0, inner, :] if axis == 0 else main_ref[0, :, inner])
        if hi > tile:
            parts.append(next_ref[0])
        return parts[0] if len(parts) == 1 else jnp.concatenate(parts, axis=axis)

    def chain(sub):
        bi, r0 = (0, sub * tq) if latent else (sub, 0)
        rows = slice(r0, r0 + tq)
        if latent:
            hb = h_ref[bi, rows, :]
        else:
            shift = mod_ref[0, 0, 0:1, :]
            scale = mod_ref[0, 0, 1:2, :]
            hb = (_layernorm(x_ref[bi]) * (1.0 + scale) + shift).astype(BF16)

        def proj(off, width):
            return jnp.dot(hb, w2_ref[0, :, off:off + width], preferred_element_type=F32)

        q = proj(O_Q, ATT_W)
        if not latent:
            kv = proj(O_K, 2 * KV_W)
            kvt = kv.T
            k_ref[bi] = kvt[:KV_W]
            v_ref[bi] = kvt[KV_W:]
            keys = [kv[:, :KV_W].astype(BF16)]
            vals = [kvt[KV_W:].astype(BF16)]
            masks = [None]
            cu = proj(O_CC, 2 * CONV_W)
            u_ctx = cu[:, :CONV_W] * cu[:, CONV_W:]
            fx = proj(O_FX, FOU_W)
            zs = [jnp.dot(fx[:, g * FOU_GW:(g + 1) * FOU_GW].astype(BF16), dftc_ref[...],
                          preferred_element_type=F32) for g in range(FOU_GROUPS)]
            zcat = jnp.concatenate([jnp.concatenate([z[:, :FOU_GW] for z in zs], axis=1),
                                    jnp.concatenate([z[:, FOU_GW:] for z in zs], axis=1)], axis=0)
            yf_ctx = jnp.dot(tpos_ref[...], zcat.astype(BF16), preferred_element_type=F32)
        if latent:
            keys = [window(km_ref, kp_ref, kn_ref, r0 - WINDOW, r0 + tq + WINDOW, 0), kc_ref[0, 0]]
            vals = [window(vm_ref, vp_ref, vn_ref, r0 - WINDOW, r0 + tq + WINDOW, 1), vc_ref[0, 0]]
            nloc = keys[0].shape[0]
            j = lax.broadcasted_iota(jnp.int32, (nloc, tq), 0)
            r = lax.broadcasted_iota(jnp.int32, (nloc, tq), 1)
            dlt = j - r
            kpos = j + (t * tile + r0 - WINDOW)
            band = (dlt >= 0) & (dlt <= 2 * WINDOW) & (kpos >= 0) & (kpos < seq_len)
            masks = [band, None]
            cos = cos_ref[rows, :]
            sin = sin_ref[rows, :]
        vals_lo, vals_hi = [], []
        for vt in vals:
            ones = jnp.ones((2 * SUBLANES, vt.shape[1]), BF16)
            vals_lo.append(jnp.concatenate([vt[:HEAD_DIM], ones], axis=0))
            vals_hi.append(jnp.concatenate([vt[HEAD_DIM:], ones], axis=0))
        lane = lax.broadcasted_iota(jnp.int32, (tq, LANES), 1)
        low = lane < HEAD_DIM
        qblocks = []
        for b in range(ATT_W // LANES):
            qb = q[:, b * LANES:(b + 1) * LANES]
            if latent:
                qb = _rope(qb, cos, sin)
            qblocks.append(qb * (HEAD_DIM ** -0.5 * LOG2E))
        heads = []
        for g in range(Q_PER_KV):
            for kv in range(KV_HEADS):
                qb = qblocks[kv * (Q_PER_KV // 2) + g // 2]
                if g % 2 != kv:
                    qb = pltpu.roll(qb, HEAD_DIM, 1)
                qpad = jnp.where(low, qb, 0.0) if kv == 0 else jnp.where(low, 0.0, qb)
                h = kv * Q_PER_KV + g
                heads.append((qpad.astype(BF16), vals_lo if kv == 0 else vals_hi, sink_ref[h:h + 1, 0:1] * LOG2E))
        yield

        gate_w = N_BRANCH * D_MODEL // 4
        side_cols = [(O_ZA, ATT_W), (O_CB, CONV_W), (O_ZB, CONV_W), (O_ZC, FOU_W)]
        side_cols += [(O_G + i * gate_w, gate_w) for i in range(4)]
        side = []
        results = []
        sc = _scores_t(heads[0][0], keys, masks)
        for i, (_, vaug, sink2) in enumerate(heads):
            nxt = _scores_t(heads[i + 1][0], keys, masks) if i + 1 < len(heads) else None
            if i < len(side_cols):
                side.append(proj(*side_cols[i]))
            results.append(_softmax_pv_t(sc, vaug, sink2))
            sc = nxt
        p_za, p_cb, p_zb, p_zc = side[:4]
        g_all = jnp.concatenate(side[4:], axis=-1)
        p_gates = [g_all[:, br * D_MODEL:(br + 1) * D_MODEL] for br in range(N_BRANCH)]
        out_t = {(i // KV_HEADS, i % KV_HEADS): o for i, o in enumerate(results)}
        ya = jnp.concatenate(
            [jnp.concatenate([out_t[(g, kv)], out_t[(g + 1, kv)]], axis=0).T
             for kv in range(KV_HEADS) for g in range(0, Q_PER_KV, 2)], axis=-1)
        y_a = ya * _silu(p_za)

        yield

        if not latent:
            u = u_ctx
            prev_row = next_row = jnp.zeros((1, CONV_W), F32)
        else:
            u = um_ref[bi, rows, :]
            if r0 == 0:
                prev_row = jnp.where(t > 0, up_ref[0, SUBLANES - 1:SUBLANES, :], 0.0)
            else:
                prev_row = um_ref[0, r0 - 1:r0, :]
            if r0 + tq == tile:
                next_row = jnp.where(t < nt - 1, un_ref[0, 0:1, :], 0.0)
            else:
                next_row = um_ref[0, r0 + tq:r0 + tq + 1, :]
        row = lax.broadcasted_iota(jnp.int32, u.shape, 0)
        u_up = jnp.where(row == 0, prev_row, pltpu.roll(u, 1, 0))
        u_dn = jnp.where(row == tq - 1, next_row, pltpu.roll(u, tq - 1, 0))
        conv = u_up * cw_ref[0, 0:1, :] + u * cw_ref[0, 1:2, :] + u_dn * cw_ref[0, 2:3, :]
        y_b = p_cb * conv * _silu(p_zb)

        y_c = (yf_ref[bi, rows, :].astype(F32) if latent else yf_ctx) * _silu(p_zc)

        merged = None
        for br, yb in enumerate((y_a, y_b, y_c)):
            gt = jax.nn.sigmoid(p_gates[br])
            term = gt * jnp.dot(yb.astype(BF16), wb_ref[0, br], preferred_element_type=F32)
            merged = term if merged is None else merged + term
        out = jnp.dot(merged.astype(BF16), wo_ref[0], preferred_element_type=F32)

        yield

        res = alpha * x_ref[bi, rows, :] + mod_ref[0, 0, 2:3, :] * out
        y_ref[bi, rows, :] = _layernorm(res) * lng_ref[0] + lnb_ref[0]
        yield

    chains = [chain(sub) for sub in range(nsub)]
    for _stage in range(4):
        for ch in chains:
            next(ch)


def _pass2(x, hb, mod, layer, w2, wb, wo, conv_w, ln_g, ln_b, sink_b, yf, u, k, v, cache_k, cache_v, rope_tabs,
           alpha):
    b, s, d = x.shape
    nsub = P2_SUB_TILES
    t = nsub * SUB_TILE
    assert s % t == 0 and SUB_TILE == 2 * WINDOW
    nt = s // t
    const3 = lambda i, j: (layer, 0, 0)
    tile3 = lambda i, j: (i, j, 0)
    rpt = t // SUBLANES
    wpt = t // WINDOW
    nb8 = s // SUBLANES
    nbw = s // WINDOW
    prev8 = lambda i, j: (i, jnp.maximum(j * rpt - 1, 0), 0)
    next8 = lambda i, j: (i, jnp.minimum((j + 1) * rpt, nb8 - 1), 0)
    prevw = lambda i, j: (i, jnp.maximum(j * wpt - 1, 0), 0)
    nextw = lambda i, j: (i, jnp.minimum((j + 1) * wpt, nbw - 1), 0)
    swap = lambda f: (lambda i, j: (f(i, j)[0], 0, f(i, j)[1]))
    past = cache_k.shape[2]
    in_specs = [
        pl.BlockSpec((1, t, d), tile3),
        pl.BlockSpec((1, t, d), tile3),
        pl.BlockSpec((1, 1, 3, d), lambda i, j: (layer, i, 0, 0)),
        pl.BlockSpec((1, d, IN_W), const3),
        pl.BlockSpec((1, N_BRANCH, BRANCH_W, d), lambda i, j: (layer, 0, 0, 0)),
        pl.BlockSpec((1, d, d), const3),
        pl.BlockSpec((1, 3, CONV_W), const3),
        pl.BlockSpec((1, 1, d), const3),
        pl.BlockSpec((1, 1, d), const3),
        pl.BlockSpec((N_HEADS, LANES), lambda i, j: (layer, 0)),
        pl.BlockSpec((1, t, FOU_W), tile3),
        pl.BlockSpec((1, t, CONV_W), tile3),
        pl.BlockSpec((1, SUBLANES, CONV_W), prev8),
        pl.BlockSpec((1, SUBLANES, CONV_W), next8),
        pl.BlockSpec((1, t, KV_W), tile3),
        pl.BlockSpec((1, WINDOW, KV_W), prevw),
        pl.BlockSpec((1, WINDOW, KV_W), nextw),
        pl.BlockSpec((1, KV_W, t), swap(tile3)),
        pl.BlockSpec((1, KV_W, WINDOW), swap(prevw)),
        pl.BlockSpec((1, KV_W, WINDOW), swap(nextw)),
        pl.BlockSpec((1, 1, past, KV_W), lambda i, j: (i, layer, 0, 0)),
        pl.BlockSpec((1, 1, KV_W, past), lambda i, j: (i, layer, 0, 0)),
        pl.BlockSpec((t, KV_W), lambda i, j: (j, 0)),
        pl.BlockSpec((t, KV_W), lambda i, j: (j, 0)),
    ]
    args = [x, hb, mod, w2, wb, wo, conv_w, ln_g, ln_b, sink_b, yf, u, u, u, k, k, k, v, v, v,
            cache_k, cache_v, *rope_tabs]
    return pl.pallas_call(
        functools.partial(_p2_kernel, True, alpha, s, nsub),
        grid=(b, nt),
        in_specs=in_specs,
        out_specs=pl.BlockSpec((1, t, d), tile3),
        out_shape=jax.ShapeDtypeStruct((b, s, d), F32),
        compiler_params=pltpu.CompilerParams(
            dimension_semantics=("parallel", "parallel"),
            vmem_limit_bytes=V7X_VMEM_BYTES * 7 // 8),
        name="pass2_latent",
    )(*args)


def _context_layer(x, mod, mod_row, layer, w2, wb, wo, conv_w, ln_g, ln_b, sink_b, dftc, tpos, alpha):
    b, s, d = x.shape
    nsub = P2_SUB_TILES
    assert s == SUB_TILE and b % nsub == 0
    const3 = lambda i, j: (layer, 0, 0)
    tile3 = lambda i, j: (i, 0, 0)
    in_specs = [
        pl.BlockSpec((nsub, s, d), tile3),
        pl.BlockSpec((1, 1, 3, d), lambda i, j: (layer, mod_row, 0, 0)),
        pl.BlockSpec((1, d, IN_W), const3),
        pl.BlockSpec((1, N_BRANCH, BRANCH_W, d), lambda i, j: (layer, 0, 0, 0)),
        pl.BlockSpec((1, d, d), const3),
        pl.BlockSpec((1, 3, CONV_W), const3),
        pl.BlockSpec((1, 1, d), const3),
        pl.BlockSpec((1, 1, d), const3),
        pl.BlockSpec((N_HEADS, LANES), lambda i, j: (layer, 0)),
        pl.BlockSpec((FOU_GW, 2 * FOU_GW), lambda i, j: (0, 0)),
        pl.BlockSpec((s, 2 * s), lambda i, j: (0, 0)),
    ]
    return pl.pallas_call(
        functools.partial(_p2_kernel, False, alpha, s, nsub),
        grid=(b // nsub, 1),
        in_specs=in_specs,
        out_specs=[pl.BlockSpec((nsub, s, d), tile3), pl.BlockSpec((nsub, KV_W, s), tile3),
                   pl.BlockSpec((nsub, KV_W, s), tile3)],
        out_shape=[jax.ShapeDtypeStruct((b, s, d), F32), jax.ShapeDtypeStruct((b, KV_W, s), F32),
                   jax.ShapeDtypeStruct((b, KV_W, s), F32)],
        compiler_params=pltpu.CompilerParams(
            dimension_semantics=("parallel", "parallel"),
            vmem_limit_bytes=V7X_VMEM_BYTES * 7 // 8),
        name="context_layer",
    )(x, mod, w2, wb, wo, conv_w, ln_g, ln_b, sink_b, dftc, tpos)


def _dft_tables(n, m):
    idx = jnp.arange(m, dtype=jnp.int32)
    r = DFT_TABLE_FACTOR
    if m <= 4 * r or m % r:
        ang = (2.0 * np.pi / n) * ((idx[:, None] * idx[None, :]) % n).astype(F32)
        return jnp.cos(ang), jnp.sin(ang)
    hi = jnp.arange(m // r, dtype=jnp.int32)
    lo = jnp.arange(r, dtype=jnp.int32)
    a = (2.0 * np.pi / n) * ((r * hi[:, None] * idx[None, :]) % n).astype(F32)
    bb = (2.0 * np.pi / n) * ((lo[:, None] * idx[None, :]) % n).astype(F32)
    ca, sa = jnp.cos(a)[:, None, :], jnp.sin(a)[:, None, :]
    cb, sb = jnp.cos(bb)[None, :, :], jnp.sin(bb)[None, :, :]
    return (ca * cb - sa * sb).reshape(m, m), (sa * cb + ca * sb).reshape(m, m)


def _position_tables(s):
    c, sn = _dft_tables(s, s // 2)
    scale = s ** -0.5
    return (c * scale).astype(BF16), (sn * scale).astype(BF16)


def _reversal_matrix():
    r = np.arange(REV_BLOCK)[:, None]
    c = np.arange(2 * REV_BLOCK)[None, :]
    return jnp.asarray(c == REV_BLOCK - r, BF16)


def _channel_matrix():
    idx = np.arange(FOU_GW)
    ang = (2.0 * np.pi / FOU_GW) * ((idx[:, None] * idx[None, :]) % FOU_GW)
    tab = np.concatenate([np.cos(ang), np.sin(ang)], axis=1) * (FOU_GW ** -0.5)
    return jnp.asarray(tab, F32).astype(BF16)


def _rope_tables(n_tokens):
    pos = np.arange(n_tokens)
    row = (pos // GRID_W).astype(np.float64)
    col = (pos % GRID_W).astype(np.float64)
    n_freq = ROT_AXIS // 2
    inv_freq = ROPE_BASE ** (-np.arange(n_freq, dtype=np.float64) / n_freq)
    lane = np.arange(LANES)
    dd = lane % HEAD_DIM
    ang = np.where((dd < ROT_AXIS)[None, :], row[:, None], col[:, None]) * inv_freq[dd % n_freq][None, :]
    sign = np.where((lane % ROT_AXIS) < n_freq, -1.0, 1.0)
    return jnp.asarray(np.cos(ang), F32), jnp.asarray(np.sin(ang) * sign[None, :], F32)


def kernel(x_prompt, x_sample, cache_k, cache_v, c, c_ctx, w_mod, b_mod, w_in, sink, conv_w, w_branch, w_o, ln_g, ln_b):
    depth = w_in.shape[0]
    alpha = float((2 * depth) ** 0.25)
    dec_b, dec_s, d = x_sample.shape
    ctx_b, ctx_s, _ = x_prompt.shape

    rows = -(-(dec_b + 1) // SUBLANES) * SUBLANES
    cond = jnp.concatenate([c, c_ctx[None, :], jnp.zeros((rows - dec_b - 1, d), F32)], axis=0)
    mod = _modulation(cond, w_mod, b_mod).reshape(depth, rows, 3, d)

    w_all = w_in.astype(BF16)
    wb = w_branch.astype(BF16)
    wo = w_o.astype(BF16)
    sink_b = jnp.broadcast_to(sink.reshape(depth * N_HEADS, 1), (depth * N_HEADS, LANES))
    ln_g3 = ln_g.reshape(depth, 1, d)
    ln_b3 = ln_b.reshape(depth, 1, d)

    dftc = _channel_matrix()
    rev = _reversal_matrix()
    rope_tabs = _rope_tables(dec_s)
    past = cache_k.shape[2]
    ck, cv = _prepare_cache(cache_k.reshape(dec_b, depth, past, KV_W), cache_v.reshape(dec_b, depth, past, KV_W))

    c_ctx_t, s_ctx_t = _dft_tables(ctx_s, ctx_s)
    tpos = (jnp.concatenate([c_ctx_t, -s_ctx_t], axis=1) * (ctx_s ** -0.5)).astype(BF16)
    y_prompt = x_prompt
    ks, vs = [], []
    for l in range(depth):
        y_prompt, k_l, v_l = _context_layer(y_prompt, mod, dec_b, l, w_all, wb, wo, conv_w, ln_g3, ln_b3, sink_b,
                                            dftc, tpos, alpha)
        ks.append(k_l)
        vs.append(v_l)
    to_out = lambda xs: jnp.stack(xs, axis=1).reshape(ctx_b, depth, KV_HEADS, HEAD_DIM, ctx_s).transpose(0, 1, 4, 2, 3)
    new_k = to_out(ks)
    new_v = to_out(vs)

    ch, sh = _position_tables(dec_s)
    y_sample = x_sample
    for l in range(depth):
        hb, kb, vt, u, z = _pass1(y_sample, mod, l, w_all, dftc, rope_tabs)
        yf = _position_dft(ch, sh, rev, z.reshape(dec_b, 2 * dec_s, FOU_W))
        y_sample = _pass2(y_sample, hb, mod, l, w_all, wb, wo, conv_w, ln_g3, ln_b3, sink_b, yf, u, kb, vt,
                          ck, cv, rope_tabs, alpha)
    return (y_prompt, y_sample, new_k, new_v)
```

```python
import functools

import numpy as np
import jax
import jax.numpy as jnp
from jax import lax
from jax.experimental import pallas as pl
from jax.experimental.pallas import tpu as pltpu

F32 = jnp.float32
BF16 = jnp.bfloat16

D_MODEL = 1024
GRID_W = 64
N_HEADS = 8
KV_HEADS = 2
HEAD_DIM = 64
Q_PER_KV = N_HEADS // KV_HEADS
ATT_W = N_HEADS * HEAD_DIM
KV_W = KV_HEADS * HEAD_DIM
WINDOW = 128
CONV_W = 512
FOU_GROUPS = 4
FOU_GW = 128
FOU_W = FOU_GROUPS * FOU_GW
BRANCH_W = 512
N_BRANCH = 3
ROT_AXIS = HEAD_DIM // 2
ROPE_BASE = 10000.0
LN_EPS = 1e-6
NEG = -1e30
LOG2E = 1.4426950408889634

_OFF = np.cumsum([0, ATT_W, KV_W, KV_W, ATT_W, CONV_W, CONV_W, CONV_W, CONV_W, FOU_W, FOU_W, N_BRANCH * D_MODEL])
(O_Q, O_K, O_V, O_ZA, O_CB, O_CC, O_CX, O_ZB, O_FX, O_ZC, O_G, IN_W) = [int(v) for v in _OFF]

V7X_VMEM_BYTES = 64 * 1024 * 1024
LANES = 128
SUBLANES = 8

P1_TOKEN_TILE = 1024
SUB_TILE = 256
P2_SUB_TILES = 2
REV_BLOCK = 128
DFT_ROW_TILE = 512
DFT_TABLE_FACTOR = 64
MOD_COL_TILE = 1536


def _layernorm(x):
    mu = jnp.mean(x, axis=-1, keepdims=True)
    xc = x - mu
    var = jnp.mean(xc * xc, axis=-1, keepdims=True)
    return xc * lax.rsqrt(var + LN_EPS)


def _silu(z):
    return z * jax.nn.sigmoid(z)


def _rope(x, cos, sin_signed):
    lane = lax.broadcasted_iota(jnp.int32, x.shape, 1)
    partner = jnp.where((lane & 16) == 0, pltpu.roll(x, LANES - 16, 1), pltpu.roll(x, 16, 1))
    return x * cos + partner * sin_signed


def _mod_kernel(c_ref, w_ref, b_ref, o_ref):
    c = c_ref[...]
    a = _silu(c).astype(BF16)
    o_ref[0] = jnp.dot(a, w_ref[0].astype(BF16), preferred_element_type=F32) + b_ref[0]


def _modulation(cond, w_mod, b_mod):
    depth, d, n = w_mod.shape
    rows = cond.shape[0]
    return pl.pallas_call(
        _mod_kernel,
        grid=(depth, n // MOD_COL_TILE),
        in_specs=[
            pl.BlockSpec((rows, d), lambda l, j: (0, 0)),
            pl.BlockSpec((1, d, MOD_COL_TILE), lambda l, j: (l, 0, j)),
            pl.BlockSpec((1, 1, MOD_COL_TILE), lambda l, j: (l, 0, j)),
        ],
        out_specs=pl.BlockSpec((1, rows, MOD_COL_TILE), lambda l, j: (l, 0, j)),
        out_shape=jax.ShapeDtypeStruct((depth, rows, n), F32),
        compiler_params=pltpu.CompilerParams(dimension_semantics=("parallel", "parallel")),
        name="modulation",
    )(cond, w_mod, b_mod.reshape(depth, 1, n))


def _p1_kernel(x_ref, mod_ref, w1_ref, dftc_ref, cos_ref, sin_ref, h_ref, kb_ref, vt_ref, u_ref, z_ref):
    shift = mod_ref[0, 0, 0:1, :]
    scale = mod_ref[0, 0, 1:2, :]
    t = x_ref.shape[1]

    def chain(rows):
        hb = (_layernorm(x_ref[0, rows, :]) * (1.0 + scale) + shift).astype(BF16)
        h_ref[0, rows, :] = hb
        yield

        def proj(lo, hi):
            return jnp.dot(hb, w1_ref[0, :, lo:hi], preferred_element_type=F32)

        fx = proj(O_FX, O_ZC)
        kv = proj(O_K, O_ZA)
        for g in range(FOU_GROUPS):
            fg = fx[:, g * FOU_GW:(g + 1) * FOU_GW].astype(BF16)
            zg = jnp.dot(fg, dftc_ref[...], preferred_element_type=F32)
            z_ref[0, 0, rows, g * FOU_GW:(g + 1) * FOU_GW] = zg[:, :FOU_GW].astype(BF16)
            z_ref[0, 1, rows, g * FOU_GW:(g + 1) * FOU_GW] = zg[:, FOU_GW:].astype(BF16)
        k = _rope(kv[:, 0:KV_W], cos_ref[rows, :], sin_ref[rows, :])
        kb_ref[0, rows, :] = k.astype(BF16)
        vt_ref[0, :, rows] = kv[:, KV_W:2 * KV_W].T.astype(BF16)
        cu = proj(O_CC, O_ZB)
        u_ref[0, rows, :] = cu[:, :CONV_W] * cu[:, CONV_W:]
        yield

    chains = [chain(slice(0, t // 2)), chain(slice(t // 2, t))]
    for _stage in range(2):
        for ch in chains:
            next(ch)


def _pass1(x, mod, layer, w1, dftc, rope_tabs):
    b, s, d = x.shape
    t = min(P1_TOKEN_TILE, s)
    assert s % t == 0
    nt = s // t
    tile3 = lambda i, j: (i, j, 0)
    in_specs = [
        pl.BlockSpec((1, t, d), tile3),
        pl.BlockSpec((1, 1, 3, d), lambda i, j: (layer, i, 0, 0)),
        pl.BlockSpec((1, d, IN_W), lambda i, j: (layer, 0, 0)),
        pl.BlockSpec((FOU_GW, 2 * FOU_GW), lambda i, j: (0, 0)),
        pl.BlockSpec((t, KV_W), lambda i, j: (j, 0)),
        pl.BlockSpec((t, KV_W), lambda i, j: (j, 0)),
    ]
    out_shape = [
        jax.ShapeDtypeStruct((b, s, d), BF16),
        jax.ShapeDtypeStruct((b, s, KV_W), BF16),
        jax.ShapeDtypeStruct((b, KV_W, s), BF16),
        jax.ShapeDtypeStruct((b, s, CONV_W), F32),
        jax.ShapeDtypeStruct((b, 2, s, FOU_W), BF16),
    ]
    out_specs = [
        pl.BlockSpec((1, t, d), tile3),
        pl.BlockSpec((1, t, KV_W), tile3),
        pl.BlockSpec((1, KV_W, t), lambda i, j: (i, 0, j)),
        pl.BlockSpec((1, t, CONV_W), tile3),
        pl.BlockSpec((1, 2, t, FOU_W), lambda i, j: (i, 0, j, 0)),
    ]
    return pl.pallas_call(
        _p1_kernel,
        grid=(b, nt),
        in_specs=in_specs,
        out_specs=out_specs,
        out_shape=out_shape,
        compiler_params=pltpu.CompilerParams(
            dimension_semantics=("parallel", "parallel"),
            vmem_limit_bytes=V7X_VMEM_BYTES * 3 // 4),
        name="pass1_latent",
    )(x, mod, w1, dftc, *rope_tabs)


def _cache_kernel(k_ref, v_ref, kb_ref, vt_ref):
    for i in range(k_ref.shape[0]):
        kb_ref[i, 0] = k_ref[i, 0].astype(BF16)
        vt_ref[i, 0] = v_ref[i, 0].T.astype(BF16)


def _prepare_cache(cache_k, cache_v):
    b, depth, past, w = cache_k.shape
    spec = pl.BlockSpec((b, 1, past, w), lambda l: (0, l, 0, 0))
    return pl.pallas_call(
        _cache_kernel,
        grid=(depth,),
        in_specs=[spec, spec],
        out_specs=[spec, pl.BlockSpec((b, 1, w, past), lambda l: (0, l, 0, 0))],
        out_shape=[jax.ShapeDtypeStruct((b, depth, past, w), BF16),
                   jax.ShapeDtypeStruct((b, depth, w, past), BF16)],
        compiler_params=pltpu.CompilerParams(dimension_semantics=("parallel",)),
        name="cache_prep",
    )(cache_k, cache_v)


def _dft_kernel(n, tm, z_ref, ch_ref, sh_ref, rev_ref, o_ref, e_ref, od_ref, ab_ref, zh_ref):
    h = n // 2
    blk = rev_ref.shape[0]
    nblk = n // blk
    mi = pl.program_id(1)
    inv = n ** -0.5
    rev = rev_ref[...]

    @pl.when(mi == 0)
    def _fold():
        r = lax.broadcasted_iota(jnp.int32, (blk, 1), 0)
        sgn = jnp.where((r & 1) == 0, 1.0, -1.0)
        asum = jnp.zeros((1, e_ref.shape[1]), F32)
        for i in range(h // blk):
            q = nblk - 1 - i
            if i == 0:
                pad = jnp.zeros((blk, e_ref.shape[1]), BF16)
                mir_r = jnp.concatenate([z_ref[0, q * blk:(q + 1) * blk, :], pad], axis=0)
                mir_i = jnp.concatenate([z_ref[0, n + q * blk:n + (q + 1) * blk, :], pad], axis=0)
            else:
                mir_r = z_ref[0, q * blk:(q + 2) * blk, :]
                mir_i = z_ref[0, n + q * blk:n + (q + 2) * blk, :]
            e = z_ref[0, i * blk:(i + 1) * blk, :].astype(F32) + jnp.dot(rev, mir_r, preferred_element_type=F32)
            o = (z_ref[0, n + i * blk:n + (i + 1) * blk, :].astype(F32)
                 - jnp.dot(rev, mir_i, preferred_element_type=F32))
            e_ref[i * blk:(i + 1) * blk, :] = e.astype(BF16)
            od_ref[i * blk:(i + 1) * blk, :] = o.astype(BF16)
            asum = asum + jnp.sum(e * sgn, axis=0, keepdims=True)
        zh = z_ref[0, h:h + 1, :].astype(F32) * inv
        zh_ref[...] = zh
        ab_ref[h:h + blk, :] = jnp.where(r == 0, asum * inv + zh, 0.0).astype(BF16)

    rows = pl.ds(pl.multiple_of(mi * tm, tm), tm)
    a = jnp.dot(ch_ref[rows, :], e_ref[...], preferred_element_type=F32)
    b = jnp.dot(sh_ref[rows, :], od_ref[...], preferred_element_type=F32)
    rr = lax.broadcasted_iota(jnp.int32, (tm, 1), 0)
    a = a + jnp.where((rr & 1) == 0, 1.0, -1.0) * zh_ref[...]
    o_ref[0, rows, :] = (a - b).astype(o_ref.dtype)
    ab_ref[rows, :] = (a + b).astype(BF16)

    @pl.when(mi == pl.num_programs(1) - 1)
    def _unfold():
        for p in range(h // blk):
            q = h // blk - p - 1
            win = ab_ref[q * blk:(q + 2) * blk, :]
            o_ref[0, h + p * blk:h + (p + 1) * blk, :] = jnp.dot(
                rev, win, preferred_element_type=F32).astype(o_ref.dtype)


def _position_dft(ch, sh, rev, z):
    b, s2, w = z.shape
    s = s2 // 2
    h = s // 2
    blk = rev.shape[0]
    assert h % blk == 0 and rev.shape[1] == 2 * blk
    tm = min(DFT_ROW_TILE, h)
    return pl.pallas_call(
        functools.partial(_dft_kernel, s, tm),
        grid=(b, h // tm),
        in_specs=[
            pl.BlockSpec((1, s2, w), lambda i, m: (i, 0, 0)),
            pl.BlockSpec((h, h), lambda i, m: (0, 0)),
            pl.BlockSpec((h, h), lambda i, m: (0, 0)),
            pl.BlockSpec((blk, 2 * blk), lambda i, m: (0, 0)),
        ],
        out_specs=pl.BlockSpec((1, s, w), lambda i, m: (i, 0, 0)),
        out_shape=jax.ShapeDtypeStruct((b, s, w), BF16),
        scratch_shapes=[pltpu.VMEM((h, w), BF16), pltpu.VMEM((h, w), BF16),
                        pltpu.VMEM((h + blk, w), BF16), pltpu.VMEM((1, w), F32)],
        compiler_params=pltpu.CompilerParams(
            dimension_semantics=("parallel", "arbitrary"),
            vmem_limit_bytes=V7X_VMEM_BYTES * 7 // 8),
        name="position_dft",
    )(z, ch, sh, rev)


def _scores_t(qpad, keys, masks):
    scores = []
    for kb, mk in zip(keys, masks):
        s = lax.dot_general(kb, qpad, (((1,), (1,)), ((), ())), preferred_element_type=F32)
        if mk is not None:
            s = jnp.where(mk, s, NEG)
        scores.append(s)
    return scores


def _softmax_pv_t(scores, vals_t, sink2):
    m = sink2
    for s in scores:
        m = jnp.maximum(m, jnp.max(s, axis=0, keepdims=True))
    acc = None
    for s, vt in zip(scores, vals_t):
        pv = jnp.dot(vt, jnp.exp2(s - m).astype(BF16), preferred_element_type=F32)
        acc = pv if acc is None else acc + pv
    den = acc[HEAD_DIM:HEAD_DIM + 1, :] + jnp.exp2(sink2 - m)
    return acc[:HEAD_DIM, :] / den


def _p2_kernel(latent, alpha, seq_len, nsub, *refs):
    if latent:
        (x_ref, h_ref, mod_ref, w2_ref, wb_ref, wo_ref, cw_ref, lng_ref, lnb_ref, sink_ref, yf_ref,
         um_ref, up_ref, un_ref, km_ref, kp_ref, kn_ref, vm_ref, vp_ref, vn_ref,
         kc_ref, vc_ref, cos_ref, sin_ref, y_ref) = refs
    else:
        (x_ref, mod_ref, w2_ref, wb_ref, wo_ref, cw_ref, lng_ref, lnb_ref, sink_ref, dftc_ref, tpos_ref,
         y_ref, k_ref, v_ref) = refs
    t = pl.program_id(1)
    nt = pl.num_programs(1)
    tq = SUB_TILE
    tile = nsub * tq if latent else tq

    def window(main_ref, prev_ref, next_ref, lo, hi, axis):
        parts = []
        if lo < 0:
            parts.append(prev_ref[0])
            lo = 0
        inner = slice(lo, min(hi, tile))
        parts.append(main_ref[0, inner, :] if axis == 0 else main_ref[0, :, inner])
        if hi > tile:
            parts.append(next_ref[0])
        return parts[0] if len(parts) == 1 else jnp.concatenate(parts, axis=axis)

    def chain(sub):
        bi, r0 = (0, sub * tq) if latent else (sub, 0)
        rows = slice(r0, r0 + tq)
        if latent:
            hb = h_ref[bi, rows, :]
        else:
            shift = mod_ref[0, 0, 0:1, :]
            scale = mod_ref[0, 0, 1:2, :]
            hb = (_layernorm(x_ref[bi]) * (1.0 + scale) + shift).astype(BF16)

        def proj(off, width):
            return jnp.dot(hb, w2_ref[0, :, off:off + width], preferred_element_type=F32)

        q = proj(O_Q, ATT_W)
        if not latent:
            kv = proj(O_K, 2 * KV_W)
            kvt = kv.T
            k_ref[bi] = kvt[:KV_W]
            v_ref[bi] = kvt[KV_W:]
            keys = [kv[:, :KV_W].astype(BF16)]
            vals = [kvt[KV_W:].astype(BF16)]
            masks = [None]
            cu = proj(O_CC, 2 * CONV_W)
            u_ctx = cu[:, :CONV_W] * cu[:, CONV_W:]
            fx = proj(O_FX, FOU_W)
            zs = [jnp.dot(fx[:, g * FOU_GW:(g + 1) * FOU_GW].astype(BF16), dftc_ref[...],
                          preferred_element_type=F32) for g in range(FOU_GROUPS)]
            zcat = jnp.concatenate([jnp.concatenate([z[:, :FOU_GW] for z in zs], axis=1),
                                    jnp.concatenate([z[:, FOU_GW:] for z in zs], axis=1)], axis=0)
            yf_ctx = jnp.dot(tpos_ref[...], zcat.astype(BF16), preferred_element_type=F32)
        if latent:
            keys = [window(km_ref, kp_ref, kn_ref, r0 - WINDOW, r0 + tq + WINDOW, 0), kc_ref[0, 0]]
            vals = [window(vm_ref, vp_ref, vn_ref, r0 - WINDOW, r0 + tq + WINDOW, 1), vc_ref[0, 0]]
            nloc = keys[0].shape[0]
            j = lax.broadcasted_iota(jnp.int32, (nloc, tq), 0)
            r = lax.broadcasted_iota(jnp.int32, (nloc, tq), 1)
            dlt = j - r
            kpos = j + (t * tile + r0 - WINDOW)
            band = (dlt >= 0) & (dlt <= 2 * WINDOW) & (kpos >= 0) & (kpos < seq_len)
            masks = [band, None]
            cos = cos_ref[rows, :]
            sin = sin_ref[rows, :]
        vals_lo, vals_hi = [], []
        for vt in vals:
            ones = jnp.ones((2 * SUBLANES, vt.shape[1]), BF16)
            vals_lo.append(jnp.concatenate([vt[:HEAD_DIM], ones], axis=0))
            vals_hi.append(jnp.concatenate([vt[HEAD_DIM:], ones], axis=0))
        lane = lax.broadcasted_iota(jnp.int32, (tq, LANES), 1)
        low = lane < HEAD_DIM
        qblocks = []
        for b in range(ATT_W // LANES):
            qb = q[:, b * LANES:(b + 1) * LANES]
            if latent:
                qb = _rope(qb, cos, sin)
            qblocks.append(qb * (HEAD_DIM ** -0.5 * LOG2E))
        heads = []
        for g in range(Q_PER_KV):
            for kv in range(KV_HEADS):
                qb = qblocks[kv * (Q_PER_KV // 2) + g // 2]
                if g % 2 != kv:
                    qb = pltpu.roll(qb, HEAD_DIM, 1)
                qpad = jnp.where(low, qb, 0.0) if kv == 0 else jnp.where(low, 0.0, qb)
                h = kv * Q_PER_KV + g
                heads.append((qpad.astype(BF16), vals_lo if kv == 0 else vals_hi, sink_ref[h:h + 1, 0:1] * LOG2E))
        yield

        gate_w = N_BRANCH * D_MODEL // 4
        side_cols = [(O_ZA, ATT_W), (O_CB, CONV_W), (O_ZB, CONV_W), (O_ZC, FOU_W)]
        side_cols += [(O_G + i * gate_w, gate_w) for i in range(4)]
        side = []
        results = []
        sc = _scores_t(heads[0][0], keys, masks)
        for i, (_, vaug, sink2) in enumerate(heads):
            nxt = _scores_t(heads[i + 1][0], keys, masks) if i + 1 < len(heads) else None
            if i < len(side_cols):
                side.append(proj(*side_cols[i]))
            results.append(_softmax_pv_t(sc, vaug, sink2))
            sc = nxt
        p_za, p_cb, p_zb, p_zc = side[:4]
        g_all = jnp.concatenate(side[4:], axis=-1)
        p_gates = [g_all[:, br * D_MODEL:(br + 1) * D_MODEL] for br in range(N_BRANCH)]
        out_t = {(i // KV_HEADS, i % KV_HEADS): o for i, o in enumerate(results)}
        ya = jnp.concatenate(
            [jnp.concatenate([out_t[(g, kv)], out_t[(g + 1, kv)]], axis=0).T
             for kv in range(KV_HEADS) for g in range(0, Q_PER_KV, 2)], axis=-1)
        y_a = ya * _silu(p_za)

        yield

        if not latent:
            u = u_ctx
            prev_row = next_row = jnp.zeros((1, CONV_W), F32)
        else:
            u = um_ref[bi, rows, :]
            if r0 == 0:
                prev_row = jnp.where(t > 0, up_ref[0, SUBLANES - 1:SUBLANES, :], 0.0)
            else:
                prev_row = um_ref[0, r0 - 1:r0, :]
            if r0 + tq == tile:
                next_row = jnp.where(t < nt - 1, un_ref[0, 0:1, :], 0.0)
            else:
                next_row = um_ref[0, r0 + tq:r0 + tq + 1, :]
        row = lax.broadcasted_iota(jnp.int32, u.shape, 0)
        u_up = jnp.where(row == 0, prev_row, pltpu.roll(u, 1, 0))
        u_dn = jnp.where(row == tq - 1, next_row, pltpu.roll(u, tq - 1, 0))
        conv = u_up * cw_ref[0, 0:1, :] + u * cw_ref[0, 1:2, :] + u_dn * cw_ref[0, 2:3, :]
        y_b = p_cb * conv * _silu(p_zb)

        y_c = (yf_ref[bi, rows, :].astype(F32) if latent else yf_ctx) * _silu(p_zc)

        merged = None
        for br, yb in enumerate((y_a, y_b, y_c)):
            gt = jax.nn.sigmoid(p_gates[br])
            term = gt * jnp.dot(yb.astype(BF16), wb_ref[0, br], preferred_element_type=F32)
            merged = term if merged is None else merged + term
        out = jnp.dot(merged.astype(BF16), wo_ref[0], preferred_element_type=F32)

        yield

        res = alpha * x_ref[bi, rows, :] + mod_ref[0, 0, 2:3, :] * out
        y_ref[bi, rows, :] = _layernorm(res) * lng_ref[0] + lnb_ref[0]
        yield

    chains = [chain(sub) for sub in range(nsub)]
    for _stage in range(4):
        for ch in chains:
            next(ch)


def _pass2(x, hb, mod, layer, w2, wb, wo, conv_w, ln_g, ln_b, sink_b, yf, u, k, v, cache_k, cache_v, rope_tabs,
           alpha):
    b, s, d = x.shape
    nsub = P2_SUB_TILES
    t = nsub * SUB_TILE
    assert s % t == 0 and SUB_TILE == 2 * WINDOW
    nt = s // t
    const3 = lambda i, j: (layer, 0, 0)
    tile3 = lambda i, j: (i, j, 0)
    rpt = t // SUBLANES
    wpt = t // WINDOW
    nb8 = s // SUBLANES
    nbw = s // WINDOW
    prev8 = lambda i, j: (i, jnp.maximum(j * rpt - 1, 0), 0)
    next8 = lambda i, j: (i, jnp.minimum((j + 1) * rpt, nb8 - 1), 0)
    prevw = lambda i, j: (i, jnp.maximum(j * wpt - 1, 0), 0)
    nextw = lambda i, j: (i, jnp.minimum((j + 1) * wpt, nbw - 1), 0)
    swap = lambda f: (lambda i, j: (f(i, j)[0], 0, f(i, j)[1]))
    past = cache_k.shape[2]
    in_specs = [
        pl.BlockSpec((1, t, d), tile3),
        pl.BlockSpec((1, t, d), tile3),
        pl.BlockSpec((1, 1, 3, d), lambda i, j: (layer, i, 0, 0)),
        pl.BlockSpec((1, d, IN_W), const3),
        pl.BlockSpec((1, N_BRANCH, BRANCH_W, d), lambda i, j: (layer, 0, 0, 0)),
        pl.BlockSpec((1, d, d), const3),
        pl.BlockSpec((1, 3, CONV_W), const3),
        pl.BlockSpec((1, 1, d), const3),
        pl.BlockSpec((1, 1, d), const3),
        pl.BlockSpec((N_HEADS, LANES), lambda i, j: (layer, 0)),
        pl.BlockSpec((1, t, FOU_W), tile3),
        pl.BlockSpec((1, t, CONV_W), tile3),
        pl.BlockSpec((1, SUBLANES, CONV_W), prev8),
        pl.BlockSpec((1, SUBLANES, CONV_W), next8),
        pl.BlockSpec((1, t, KV_W), tile3),
        pl.BlockSpec((1, WINDOW, KV_W), prevw),
        pl.BlockSpec((1, WINDOW, KV_W), nextw),
        pl.BlockSpec((1, KV_W, t), swap(tile3)),
        pl.BlockSpec((1, KV_W, WINDOW), swap(prevw)),
        pl.BlockSpec((1, KV_W, WINDOW), swap(nextw)),
        pl.BlockSpec((1, 1, past, KV_W), lambda i, j: (i, layer, 0, 0)),
        pl.BlockSpec((1, 1, KV_W, past), lambda i, j: (i, layer, 0, 0)),
        pl.BlockSpec((t, KV_W), lambda i, j: (j, 0)),
        pl.BlockSpec((t, KV_W), lambda i, j: (j, 0)),
    ]
    args = [x, hb, mod, w2, wb, wo, conv_w, ln_g, ln_b, sink_b, yf, u, u, u, k, k, k, v, v, v,
            cache_k, cache_v, *rope_tabs]
    return pl.pallas_call(
        functools.partial(_p2_kernel, True, alpha, s, nsub),
        grid=(b, nt),
        in_specs=in_specs,
        out_specs=pl.BlockSpec((1, t, d), tile3),
        out_shape=jax.ShapeDtypeStruct((b, s, d), F32),
        compiler_params=pltpu.CompilerParams(
            dimension_semantics=("parallel", "parallel"),
            vmem_limit_bytes=V7X_VMEM_BYTES * 7 // 8),
        name="pass2_latent",
    )(*args)


def _context_layer(x, mod, mod_row, layer, w2, wb, wo, conv_w, ln_g, ln_b, sink_b, dftc, tpos, alpha):
    b, s, d = x.shape
    nsub = P2_SUB_TILES
    assert s == SUB_TILE and b % nsub == 0
    const3 = lambda i, j: (layer, 0, 0)
    tile3 = lambda i, j: (i, 0, 0)
    in_specs = [
        pl.BlockSpec((nsub, s, d), tile3),
        pl.BlockSpec((1, 1, 3, d), lambda i, j: (layer, mod_row, 0, 0)),
        pl.BlockSpec((1, d, IN_W), const3),
        pl.BlockSpec((1, N_BRANCH, BRANCH_W, d), lambda i, j: (layer, 0, 0, 0)),
        pl.BlockSpec((1, d, d), const3),
        pl.BlockSpec((1, 3, CONV_W), const3),
        pl.BlockSpec((1, 1, d), const3),
        pl.BlockSpec((1, 1, d), const3),
        pl.BlockSpec((N_HEADS, LANES), lambda i, j: (layer, 0)),
        pl.BlockSpec((FOU_GW, 2 * FOU_GW), lambda i, j: (0, 0)),
        pl.BlockSpec((s, 2 * s), lambda i, j: (0, 0)),
    ]
    return pl.pallas_call(
        functools.partial(_p2_kernel, False, alpha, s, nsub),
        grid=(b // nsub, 1),
        in_specs=in_specs,
        out_specs=[pl.BlockSpec((nsub, s, d), tile3), pl.BlockSpec((nsub, KV_W, s), tile3),
                   pl.BlockSpec((nsub, KV_W, s), tile3)],
        out_shape=[jax.ShapeDtypeStruct((b, s, d), F32), jax.ShapeDtypeStruct((b, KV_W, s), F32),
                   jax.ShapeDtypeStruct((b, KV_W, s), F32)],
        compiler_params=pltpu.CompilerParams(
            dimension_semantics=("parallel", "parallel"),
            vmem_limit_bytes=V7X_VMEM_BYTES * 7 // 8),
        name="context_layer",
    )(x, mod, w2, wb, wo, conv_w, ln_g, ln_b, sink_b, dftc, tpos)


def _dft_tables(n, m):
    idx = jnp.arange(m, dtype=jnp.int32)
    r = DFT_TABLE_FACTOR
    if m <= 4 * r or m % r:
        ang = (2.0 * np.pi / n) * ((idx[:, None] * idx[None, :]) % n).astype(F32)
        return jnp.cos(ang), jnp.sin(ang)
    hi = jnp.arange(m // r, dtype=jnp.int32)
    lo = jnp.arange(r, dtype=jnp.int32)
    a = (2.0 * np.pi / n) * ((r * hi[:, None] * idx[None, :]) % n).astype(F32)
    bb = (2.0 * np.pi / n) * ((lo[:, None] * idx[None, :]) % n).astype(F32)
    ca, sa = jnp.cos(a)[:, None, :], jnp.sin(a)[:, None, :]
    cb, sb = jnp.cos(bb)[None, :, :], jnp.sin(bb)[None, :, :]
    return (ca * cb - sa * sb).reshape(m, m), (sa * cb + ca * sb).reshape(m, m)


def _position_tables(s):
    c, sn = _dft_tables(s, s // 2)
    scale = s ** -0.5
    return (c * scale).astype(BF16), (sn * scale).astype(BF16)


def _reversal_matrix():
    r = np.arange(REV_BLOCK)[:, None]
    c = np.arange(2 * REV_BLOCK)[None, :]
    return jnp.asarray(c == REV_BLOCK - r, BF16)


def _channel_matrix():
    idx = np.arange(FOU_GW)
    ang = (2.0 * np.pi / FOU_GW) * ((idx[:, None] * idx[None, :]) % FOU_GW)
    tab = np.concatenate([np.cos(ang), np.sin(ang)], axis=1) * (FOU_GW ** -0.5)
    return jnp.asarray(tab, F32).astype(BF16)


def _rope_tables(n_tokens):
    pos = np.arange(n_tokens)
    row = (pos // GRID_W).astype(np.float64)
    col = (pos % GRID_W).astype(np.float64)
    n_freq = ROT_AXIS // 2
    inv_freq = ROPE_BASE ** (-np.arange(n_freq, dtype=np.float64) / n_freq)
    lane = np.arange(LANES)
    dd = lane % HEAD_DIM
    ang = np.where((dd < ROT_AXIS)[None, :], row[:, None], col[:, None]) * inv_freq[dd % n_freq][None, :]
    sign = np.where((lane % ROT_AXIS) < n_freq, -1.0, 1.0)
    return jnp.asarray(np.cos(ang), F32), jnp.asarray(np.sin(ang) * sign[None, :], F32)


def kernel(x_prompt, x_sample, cache_k, cache_v, c, c_ctx, w_mod, b_mod, w_in, sink, conv_w, w_branch, w_o, ln_g, ln_b):
    depth = w_in.shape[0]
    alpha = float((2 * depth) ** 0.25)
    dec_b, dec_s, d = x_sample.shape
    ctx_b, ctx_s, _ = x_prompt.shape

    rows = -(-(dec_b + 1) // SUBLANES) * SUBLANES
    cond = jnp.concatenate([c, c_ctx[None, :], jnp.zeros((rows - dec_b - 1, d), F32)], axis=0)
    mod = _modulation(cond, w_mod, b_mod).reshape(depth, rows, 3, d)

    w_all = w_in.astype(BF16)
    wb = w_branch.astype(BF16)
    wo = w_o.astype(BF16)
    sink_b = jnp.broadcast_to(sink.reshape(depth * N_HEADS, 1), (depth * N_HEADS, LANES))
    ln_g3 = ln_g.reshape(depth, 1, d)
    ln_b3 = ln_b.reshape(depth, 1, d)

    dftc = _channel_matrix()
    rev = _reversal_matrix()
    rope_tabs = _rope_tables(dec_s)
    past = cache_k.shape[2]
    ck, cv = _prepare_cache(cache_k.reshape(dec_b, depth, past, KV_W), cache_v.reshape(dec_b, depth, past, KV_W))

    c_ctx_t, s_ctx_t = _dft_tables(ctx_s, ctx_s)
    tpos = (jnp.concatenate([c_ctx_t, -s_ctx_t], axis=1) * (ctx_s ** -0.5)).astype(BF16)
    y_prompt = x_prompt
    ks, vs = [], []
    for l in range(depth):
        y_prompt, k_l, v_l = _context_layer(y_prompt, mod, dec_b, l, w_all, wb, wo, conv_w, ln_g3, ln_b3, sink_b,
                                            dftc, tpos, alpha)
        ks.append(k_l)
        vs.append(v_l)
    to_out = lambda xs: jnp.stack(xs, axis=1).reshape(ctx_b, depth, KV_HEADS, HEAD_DIM, ctx_s).transpose(0, 1, 4, 2, 3)
    new_k = to_out(ks)
    new_v = to_out(vs)

    ch, sh = _position_tables(dec_s)
    y_sample = x_sample
    for l in range(depth):
        hb, kb, vt, u, z = _pass1(y_sample, mod, l, w_all, dftc, rope_tabs)
        yf = _position_dft(ch, sh, rev, z.reshape(dec_b, 2 * dec_s, FOU_W))
        y_sample = _pass2(y_sample, hb, mod, l, w_all, wb, wo, conv_w, ln_g3, ln_b3, sink_b, yf, u, kb, vt,
                          ck, cv, rope_tabs, alpha)
    return (y_prompt, y_sample, new_k, new_v)
```
